```python
import jax, jax.numpy as jnp
from jax import lax
import numpy as np

D_MODEL = 4096
BATCH = 4
SEQ = 2048
DEPTH = 1
DEC_BATCH = 128
DEC_SEQ = 1
PAST_LEN = 16384
PAGE_SIZE = 128

GDN_HEADS = 16
GDN_DK = 128
GDN_DV = 128
GDN_CONV = 4
GDN_CHUNK = 64
GDN_QK_W = GDN_HEADS * GDN_DK
GDN_V_W = GDN_HEADS * GDN_DV
GDN_CONV_W = 2 * GDN_QK_W + GDN_V_W

MLA_HEADS = 16
MLA_Q_RANK = 768
MLA_KV_RANK = 512
MLA_NOPE = 128
MLA_ROPE = 64
MLA_QK = MLA_NOPE + MLA_ROPE
MLA_VDIM = 128
MLA_V_W = MLA_HEADS * MLA_VDIM
ROPE_THETA = 10000.0
SM_SCALE = MLA_QK ** -0.5
ATTN_QBLOCK = 128

PEER_KEYS = 128
PEER_EXPERTS = PEER_KEYS * PEER_KEYS
PEER_HEADS = 8
PEER_TOPK = 16
PEER_QDIM = 256
PEER_BLOCK = 128

EPS = 1e-6

IN_SIZES = (GDN_CONV_W, GDN_HEADS, GDN_HEADS, GDN_V_W, MLA_Q_RANK, MLA_KV_RANK, MLA_ROPE, D_MODEL, D_MODEL)
IN_COLS = sum(IN_SIZES)

kernel_name = 'hybrid_gdn_mla_peer_step'


def rmsnorm(x, g):
    xf = x.astype(jnp.float32)
    y = xf * lax.rsqrt(jnp.mean(xf * xf, axis=-1, keepdims=True) + EPS)
    return (y * g.astype(jnp.float32)).astype(x.dtype)


def l2norm(x):
    xf = x.astype(jnp.float32)
    return xf * lax.rsqrt(jnp.sum(xf * xf, axis=-1, keepdims=True) + EPS)


def apply_rope(x, pos):
    half = MLA_ROPE // 2
    inv = jnp.power(ROPE_THETA, -2.0 * jnp.arange(half, dtype=jnp.float32) / MLA_ROPE)
    ang = pos.astype(jnp.float32)[:, None] * inv[None, :]
    cos = jnp.cos(ang)[None, :, None, :]
    sin = jnp.sin(ang)[None, :, None, :]
    xf = x.astype(jnp.float32)
    x1, x2 = xf[..., :half], xf[..., half:]
    return jnp.concatenate([x1 * cos - x2 * sin, x2 * cos + x1 * sin], axis=-1).astype(x.dtype)


def split_in(z):
    parts = []
    off = 0
    for n in IN_SIZES:
        parts.append(z[..., off:off + n])
        off += n
    return parts


def causal_conv(xp, w):
    s = xp.shape[1] - (GDN_CONV - 1)
    acc = xp[:, 0:s] * w[0]
    for j in range(1, GDN_CONV):
        acc = acc + xp[:, j:j + s] * w[j]
    return jax.nn.silu(acc)


def gdn_features(conv_out, a_logit, b_logit, gdn_a_log, gdn_dt_bias):
    b, s, _ = conv_out.shape
    q = l2norm(conv_out[..., :GDN_QK_W].reshape(b, s, GDN_HEADS, GDN_DK)) * (GDN_DK ** -0.5)
    k = l2norm(conv_out[..., GDN_QK_W:2 * GDN_QK_W].reshape(b, s, GDN_HEADS, GDN_DK))
    v = conv_out[..., 2 * GDN_QK_W:].reshape(b, s, GDN_HEADS, GDN_DV).astype(jnp.float32)
    beta = jax.nn.sigmoid(b_logit.astype(jnp.float32))
    g = -jnp.exp(gdn_a_log.astype(jnp.float32)) * jax.nn.softplus(
        a_logit.astype(jnp.float32) + gdn_dt_bias.astype(jnp.float32))
    return q, k, v, g, beta


def gdn_chunked(q, k, v, g, beta, state0):
    b, s, h, _ = q.shape
    n = s // GDN_CHUNK

    def to_chunks(t):
        t = t.reshape((b, n, GDN_CHUNK, h) + t.shape[3:])
        return jnp.moveaxis(t, (1, 3), (0, 2))

    qc, kc, vc = to_chunks(q), to_chunks(k), to_chunks(v)
    gc = jnp.cumsum(to_chunks(g), axis=-1)
    bc = to_chunks(beta)
    idx = jnp.arange(GDN_CHUNK)
    causal = idx[:, None] >= idx[None, :]
    strict = idx[:, None] > idx[None, :]
    decay = jnp.exp(jnp.where(causal, gc[..., :, None] - gc[..., None, :], -jnp.inf))
    kb = kc * bc[..., None]
    lower = jnp.where(strict, jnp.einsum('nbhid,nbhjd->nbhij', kb, kc) * decay, 0.0)
    eye = jnp.eye(GDN_CHUNK, dtype=jnp.float32)
    tmat = lax.linalg.triangular_solve(eye + lower, jnp.broadcast_to(eye, lower.shape),
                                       left_side=True, lower=True, unit_diagonal=True)
    u = jnp.einsum('nbhij,nbhjd->nbhid', tmat, vc * bc[..., None])
    w = jnp.einsum('nbhij,nbhjd->nbhid', tmat, kb * jnp.exp(gc)[..., None])
    qk = jnp.where(causal, jnp.einsum('nbhid,nbhjd->nbhij', qc, kc) * decay, 0.0)

    def step(st, inp):
        qi, ki, ui, wi, gi, ai = inp
        v_new = ui - jnp.einsum('bhcd,bhdv->bhcv', wi, st)
        o = jnp.einsum('bhcd,bhdv->bhcv', qi * jnp.exp(gi)[..., None], st) + jnp.einsum('bhij,bhjv->bhiv', ai, v_new)
        g_last = gi[..., -1]
        st = st * jnp.exp(g_last)[..., None, None] + jnp.einsum(
            'bhcd,bhcv->bhdv', ki * jnp.exp(g_last[..., None] - gi)[..., None], v_new)
        return st, o

    s_fin, o = lax.scan(step, state0, (qc, kc, u, w, gc, qk))
    o = jnp.moveaxis(o, (0, 2), (1, 3)).reshape(b, s, h, GDN_DV)
    return o, s_fin


def gdn_recurrent(q, k, v, g, beta, state0):
    def step(st, inp):
        qt, kt, vt, gt, bt = inp
        st = st * jnp.exp(gt)[..., None, None]
        delta = (vt - jnp.einsum('bhd,bhdv->bhv', kt, st)) * bt[..., None]
        st = st + jnp.einsum('bhd,bhv->bhdv', kt, delta)
        return st, jnp.einsum('bhd,bhdv->bhv', qt, st)

    xs = (jnp.moveaxis(q, 1, 0), jnp.moveaxis(k, 1, 0), jnp.moveaxis(v, 1, 0),
          jnp.moveaxis(g, 1, 0), jnp.moveaxis(beta, 1, 0))
    s_fin, o = lax.scan(step, state0, xs)
    return jnp.moveaxis(o, 0, 1), s_fin


def gdn_output(o, z_gate, gdn_norm_g):
    b, s = o.shape[:2]
    z = z_gate.reshape(b, s, GDN_HEADS, GDN_DV).astype(jnp.float32)
    return (rmsnorm(o, gdn_norm_g) * jax.nn.silu(z)).reshape(b, s, GDN_V_W)


def mla_queries(c_q, pos, mla_q_norm_g, mla_w_uq, mla_qk_norm_q):
    b, s, _ = c_q.shape
    q = (rmsnorm(c_q, mla_q_norm_g) @ mla_w_uq).reshape(b, s, MLA_HEADS, MLA_QK)
    q = rmsnorm(q, mla_qk_norm_q)
    return jnp.concatenate([q[..., :MLA_NOPE], apply_rope(q[..., MLA_NOPE:], pos)], axis=-1)


def mla_keys(c_kv, k_rope, pos, mla_w_uk, mla_qk_norm_k):
    b, s, _ = c_kv.shape
    k_nope = jnp.einsum('bsc,chd->bshd', c_kv, mla_w_uk)
    kr = jnp.broadcast_to(k_rope[:, :, None, :], (b, s, MLA_HEADS, MLA_ROPE)).astype(k_nope.dtype)
    k = rmsnorm(jnp.concatenate([k_nope, kr], axis=-1), mla_qk_norm_k)
    return jnp.concatenate([k[..., :MLA_NOPE], apply_rope(k[..., MLA_NOPE:], pos)], axis=-1)


def mla_prompt_attention(q, k, v):
    b, s, h, _ = q.shape
    nq = s // ATTN_QBLOCK
    qb = jnp.moveaxis(q.reshape(b, nq, ATTN_QBLOCK, h, MLA_QK), 1, 0)
    kpos = jnp.arange(s)
    kf = k.astype(jnp.float32)
    vf = v.astype(jnp.float32)

    def block(inp):
        qi, bi = inp
        sc = jnp.einsum('bqhd,bkhd->bhqk', qi.astype(jnp.float32), kf) * SM_SCALE
        qpos = bi * ATTN_QBLOCK + jnp.arange(ATTN_QBLOCK)
        sc = jnp.where(kpos[None, :] <= qpos[:, None], sc, -jnp.inf)
        p = jax.nn.softmax(sc, axis=-1)
        return jnp.einsum('bhqk,bkhd->bqhd', p, vf)

    o = lax.map(block, (qb, jnp.arange(nq)))
    return jnp.moveaxis(o, 0, 1).reshape(b, s, h, MLA_VDIM)


def mla_sample_attention(q, c_new, k_new, cache_latent, cache_krope, layer, page_table,
                         mla_w_uk, mla_w_uv, mla_qk_norm_k):
    t = q.shape[1]
    qf = q.astype(jnp.float32)
    sc = jnp.einsum('bqhd,bkhd->bhqk', qf, k_new.astype(jnp.float32)) * SM_SCALE
    tri = jnp.arange(t)[:, None] >= jnp.arange(t)[None, :]
    sc = jnp.where(tri, sc, -jnp.inf)
    m = jnp.max(sc, axis=-1)
    p = jnp.exp(sc - m[..., None])
    l = jnp.sum(p, axis=-1)
    acc = jnp.einsum('bhqk,bkc->bhqc', p, c_new.astype(jnp.float32))
    n_pages = page_table.shape[1]

    def step(carry, inp):
        m, l, acc = carry
        phys, pidx = inp
        c = cache_latent[layer, phys]
        kr = cache_krope[layer, phys]
        pos = pidx * PAGE_SIZE + jnp.arange(PAGE_SIZE, dtype=jnp.int32)
        kp = mla_keys(c, kr, pos, mla_w_uk, mla_qk_norm_k)
        sp = jnp.einsum('bqhd,bkhd->bhqk', qf, kp.astype(jnp.float32)) * SM_SCALE
        m_new = jnp.maximum(m, jnp.max(sp, axis=-1))
        corr = jnp.exp(m - m_new)
        pp = jnp.exp(sp - m_new[..., None])
        l = l * corr + jnp.sum(pp, axis=-1)
        acc = acc * corr[..., None] + jnp.einsum('bhqk,bkc->bhqc', pp, c.astype(jnp.float32))
        return (m_new, l, acc), None

    (m, l, acc), _ = lax.scan(step, (m, l, acc), (page_table.T, jnp.arange(n_pages, dtype=jnp.int32)))
    lat = acc / l[..., None]
    return jnp.einsum('bhqc,chd->bqhd', lat, mla_w_uv.astype(jnp.float32))


def peer_ffn(x, peer_w_q, peer_subkeys, peer_u, peer_v):
    t = x.shape[0]
    nb = -(-t // PEER_BLOCK)
    xp = jnp.pad(x, ((0, nb * PEER_BLOCK - t), (0, 0))).reshape(nb, PEER_BLOCK, D_MODEL)
    cand_n = PEER_TOPK * PEER_TOPK

    def block(xt):
        q = (xt @ peer_w_q).reshape(PEER_BLOCK, PEER_HEADS, 2, PEER_QDIM // 2).astype(jnp.float32)
        sc = jnp.einsum('thpd,hpkd->thpk', q, peer_subkeys.astype(jnp.float32))
        s1, i1 = lax.top_k(sc[:, :, 0], PEER_TOPK)
        s2, i2 = lax.top_k(sc[:, :, 1], PEER_TOPK)
        cand = (s1[..., :, None] + s2[..., None, :]).reshape(PEER_BLOCK, PEER_HEADS, cand_n)
        cidx = (i1[..., :, None] * PEER_KEYS + i2[..., None, :]).reshape(PEER_BLOCK, PEER_HEADS, cand_n)
        best, sel = lax.top_k(cand, PEER_TOPK)
        eidx = jnp.take_along_axis(cidx, sel, axis=-1)
        gate = jax.nn.softmax(best, axis=-1)
        act = jax.nn.gelu(jnp.einsum('td,thkd->thk', xt, peer_u[eidx]).astype(jnp.float32), approximate=False)
        return jnp.einsum('thk,thkd->td', (gate * act).astype(x.dtype), peer_v[eidx])

    return lax.map(block, xp).reshape(nb * PEER_BLOCK, D_MODEL)[:t]


def finish_layer(x, gate_a, gate_b, o_gdn, o_mla, gdn_out, mla_out, w_o, norm_ffn_g,
                 peer_w_q, peer_subkeys, peer_u, peer_v):
    b, s, _ = x.shape
    ya = o_gdn.astype(x.dtype) @ gdn_out
    yb = o_mla.reshape(b, s, MLA_V_W).astype(x.dtype) @ mla_out
    merged = jax.nn.sigmoid(gate_a) * ya + jax.nn.sigmoid(gate_b) * yb
    x = x + merged @ w_o
    hf = rmsnorm(x, norm_ffn_g).reshape(b * s, D_MODEL)
    return x + peer_ffn(hf, peer_w_q, peer_subkeys, peer_u, peer_v).reshape(b, s, D_MODEL)


def prompt_layer(x, norm_mix_g, w_in, gdn_conv_w, gdn_a_log, gdn_dt_bias, gdn_norm_g, gdn_out,
                 mla_q_norm_g, mla_kv_norm_g, mla_w_uq, mla_w_uk, mla_w_uv, mla_qk_norm_q, mla_qk_norm_k,
                 mla_out, w_o, norm_ffn_g, peer_w_q, peer_subkeys, peer_u, peer_v):
    b, s, _ = x.shape
    qkv_in, a_logit, b_logit, z_gate, c_q, c_kv_raw, k_rope, gate_a, gate_b = split_in(rmsnorm(x, norm_mix_g) @ w_in)
    xp = jnp.concatenate([jnp.zeros((b, GDN_CONV - 1, GDN_CONV_W), x.dtype), qkv_in], axis=1)
    q, k, v, g, beta = gdn_features(causal_conv(xp, gdn_conv_w), a_logit, b_logit, gdn_a_log, gdn_dt_bias)
    o_gdn, s_gdn = gdn_chunked(q, k, v, g, beta, jnp.zeros((b, GDN_HEADS, GDN_DK, GDN_DV), jnp.float32))
    pos = jnp.arange(s, dtype=jnp.int32)
    c_kv = rmsnorm(c_kv_raw, mla_kv_norm_g)
    qm = mla_queries(c_q, pos, mla_q_norm_g, mla_w_uq, mla_qk_norm_q)
    km = mla_keys(c_kv, k_rope, pos, mla_w_uk, mla_qk_norm_k)
    vm = jnp.einsum('bsc,chd->bshd', c_kv, mla_w_uv)
    o_mla = mla_prompt_attention(qm, km, vm)
    y = finish_layer(x, gate_a, gate_b, gdn_output(o_gdn, z_gate, gdn_norm_g), o_mla, gdn_out, mla_out, w_o,
                     norm_ffn_g, peer_w_q, peer_subkeys, peer_u, peer_v)
    return y, c_kv, k_rope, s_gdn, xp[:, -(GDN_CONV - 1):]


def sample_layer(x, cache_latent, cache_krope, layer, page_table, st_gdn, st_conv,
                 norm_mix_g, w_in, gdn_conv_w, gdn_a_log, gdn_dt_bias, gdn_norm_g, gdn_out,
                 mla_q_norm_g, mla_kv_norm_g, mla_w_uq, mla_w_uk, mla_w_uv, mla_qk_norm_q, mla_qk_norm_k,
                 mla_out, w_o, norm_ffn_g, peer_w_q, peer_subkeys, peer_u, peer_v):
    t = x.shape[1]
    qkv_in, a_logit, b_logit, z_gate, c_q, c_kv_raw, k_rope, gate_a, gate_b = split_in(rmsnorm(x, norm_mix_g) @ w_in)
    xp = jnp.concatenate([st_conv.astype(x.dtype), qkv_in], axis=1)
    q, k, v, g, beta = gdn_features(causal_conv(xp, gdn_conv_w), a_logit, b_logit, gdn_a_log, gdn_dt_bias)
    o_gdn, s_gdn = gdn_recurrent(q, k, v, g, beta, st_gdn.astype(jnp.float32))
    past_len = page_table.shape[1] * PAGE_SIZE
    pos = past_len + jnp.arange(t, dtype=jnp.int32)
    c_kv = rmsnorm(c_kv_raw, mla_kv_norm_g)
    qm = mla_queries(c_q, pos, mla_q_norm_g, mla_w_uq, mla_qk_norm_q)
    km = mla_keys(c_kv, k_rope, pos, mla_w_uk, mla_qk_norm_k)
    o_mla = mla_sample_attention(qm, c_kv, km, cache_latent, cache_krope, layer, page_table,
                                 mla_w_uk, mla_w_uv, mla_qk_norm_k)
    y = finish_layer(x, gate_a, gate_b, gdn_output(o_gdn, z_gate, gdn_norm_g), o_mla, gdn_out, mla_out, w_o,
                     norm_ffn_g, peer_w_q, peer_subkeys, peer_u, peer_v)
    return y, c_kv, k_rope, s_gdn, xp[:, -(GDN_CONV - 1):]


def setup_inputs(seed: int = 0) -> dict:
    key = jax.random.key(seed)
    ks = jax.random.split(key, 32)
    n_pages = PAST_LEN // PAGE_SIZE
    n_used = DEC_BATCH * n_pages
    n_pool = n_used + n_used // 4 + 1
    f32 = jnp.float32

    def nrm(k, shape, scale):
        return jax.random.normal(k, shape, f32) * scale

    def gain(k, n):
        return 1.0 + 0.01 * jax.random.normal(k, (DEPTH, n), f32)

    page_table = jax.random.permutation(ks[4], n_pool)[:n_used].reshape(DEC_BATCH, n_pages).astype(jnp.int32)
    dt = jnp.exp(jax.random.uniform(ks[10], (DEPTH, GDN_HEADS), f32, np.log(1e-3), np.log(1e-1)))
    return {
        'x_prompt': nrm(ks[0], (BATCH, SEQ, D_MODEL), 1.0),
        'x_sample': nrm(ks[1], (DEC_BATCH, DEC_SEQ, D_MODEL), 1.0),
        'cache_latent': nrm(ks[2], (DEPTH, n_pool, PAGE_SIZE, MLA_KV_RANK), 1.0),
        'cache_krope': nrm(ks[3], (DEPTH, n_pool, PAGE_SIZE, MLA_ROPE), 1.0),
        'page_table': page_table,
        'state_gdn': nrm(ks[5], (DEPTH, DEC_BATCH, GDN_HEADS, GDN_DK, GDN_DV), 0.1),
        'state_conv': nrm(ks[6], (DEPTH, DEC_BATCH, GDN_CONV - 1, GDN_CONV_W), 1.0),
        'norm_mix_g': gain(ks[7], D_MODEL),
        'w_in': nrm(ks[8], (DEPTH, D_MODEL, IN_COLS), D_MODEL ** -0.5),
        'gdn_conv_w': nrm(ks[9], (DEPTH, GDN_CONV, GDN_CONV_W), GDN_CONV ** -0.5),
        'gdn_a_log': jnp.log(jax.random.uniform(ks[11], (DEPTH, GDN_HEADS), f32, 1.0, 16.0)),
        'gdn_dt_bias': dt + jnp.log(-jnp.expm1(-dt)),
        'gdn_norm_g': gain(ks[12], GDN_DV),
        'gdn_out': nrm(ks[13], (DEPTH, GDN_V_W, D_MODEL), GDN_V_W ** -0.5),
        'mla_q_norm_g': gain(ks[14], MLA_Q_RANK),
        'mla_kv_norm_g': gain(ks[15], MLA_KV_RANK),
        'mla_w_uq': nrm(ks[16], (DEPTH, MLA_Q_RANK, MLA_HEADS * MLA_QK), MLA_Q_RANK ** -0.5),
        'mla_w_uk': nrm(ks[17], (DEPTH, MLA_KV_RANK, MLA_HEADS, MLA_NOPE), MLA_KV_RANK ** -0.5),
        'mla_w_uv': nrm(ks[18], (DEPTH, MLA_KV_RANK, MLA_HEADS, MLA_VDIM), MLA_KV_RANK ** -0.5),
        'mla_qk_norm_q': gain(ks[19], MLA_QK),
        'mla_qk_norm_k': gain(ks[20], MLA_QK),
        'mla_out': nrm(ks[21], (DEPTH, MLA_V_W, D_MODEL), MLA_V_W ** -0.5),
        'w_o': nrm(ks[22], (DEPTH, D_MODEL, D_MODEL), D_MODEL ** -0.5),
        'norm_ffn_g': gain(ks[23], D_MODEL),
        'peer_w_q': nrm(ks[24], (DEPTH, D_MODEL, PEER_HEADS * PEER_QDIM), D_MODEL ** -0.5),
        'peer_subkeys': nrm(ks[25], (DEPTH, PEER_HEADS, 2, PEER_KEYS, PEER_QDIM // 2), (PEER_QDIM // 2) ** -0.5),
        'peer_u': nrm(ks[26], (DEPTH, PEER_EXPERTS, D_MODEL), D_MODEL ** -0.5),
        'peer_v': nrm(ks[27], (DEPTH, PEER_EXPERTS, D_MODEL), PEER_HEADS ** -0.5),
    }


def reference(x_prompt, x_sample, cache_latent, cache_krope, page_table, state_gdn, state_conv,
              norm_mix_g, w_in, gdn_conv_w, gdn_a_log, gdn_dt_bias, gdn_norm_g, gdn_out,
              mla_q_norm_g, mla_kv_norm_g, mla_w_uq, mla_w_uk, mla_w_uv, mla_qk_norm_q, mla_qk_norm_k,
              mla_out, w_o, norm_ffn_g, peer_w_q, peer_subkeys, peer_u, peer_v):
    weights = (norm_mix_g, w_in, gdn_conv_w, gdn_a_log, gdn_dt_bias, gdn_norm_g, gdn_out,
               mla_q_norm_g, mla_kv_norm_g, mla_w_uq, mla_w_uk, mla_w_uv, mla_qk_norm_q, mla_qk_norm_k,
               mla_out, w_o, norm_ffn_g, peer_w_q, peer_subkeys, peer_u, peer_v)
    y_p, y_s = x_prompt, x_sample
    lat_p, kr_p, st_p, cv_p = [], [], [], []
    lat_s, kr_s, st_s, cv_s = [], [], [], []
    for layer in range(DEPTH):
        lw = [w[layer] for w in weights]
        y_p, c1, r1, s1, v1 = prompt_layer(y_p, *lw)
        y_s, c2, r2, s2, v2 = sample_layer(y_s, cache_latent, cache_krope, layer, page_table,
                                           state_gdn[layer], state_conv[layer], *lw)
        lat_p.append(c1); kr_p.append(r1); st_p.append(s1.astype(state_gdn.dtype)); cv_p.append(v1)
        lat_s.append(c2); kr_s.append(r2); st_s.append(s2.astype(state_gdn.dtype)); cv_s.append(v2)
    return (y_p, y_s, jnp.stack(lat_p), jnp.stack(kr_p), jnp.stack(lat_s), jnp.stack(kr_s),
            jnp.stack(st_p), jnp.stack(st_s), jnp.stack(cv_p), jnp.stack(cv_s))
```

```python
import functools
import math

import jax
import jax.numpy as jnp
from jax import lax
from jax.experimental import pallas as pl
from jax.experimental.pallas import tpu as pltpu

F32 = jnp.float32
BF16 = jnp.bfloat16

EPS = 1e-6
ROPE_THETA = 10000.0
GDN_CHUNK = 64
PEER_TOPK = 16
LANES = 128
VMEM_LIMIT = 56 * 1024 * 1024


def _cp(dims, vmem=VMEM_LIMIT):
    return pltpu.CompilerParams(dimension_semantics=dims, vmem_limit_bytes=vmem)


def _pick(n, cands):
    for c in cands:
        if c <= n and n % c == 0:
            return c
    return n


def _nt(a, b, precision=None):
    return lax.dot_general(a, b, (((1,), (1,)), ((), ())),
                           preferred_element_type=F32, precision=precision)


def _tn(a, b):
    return lax.dot_general(a, b, (((0,), (0,)), ((), ())), preferred_element_type=F32)


def _dot(a, b, precision=None):
    return jnp.dot(a, b, preferred_element_type=F32, precision=precision)


def _softplus(x):
    return jnp.maximum(x, 0.0) + jnp.log1p(jnp.exp(-jnp.abs(x)))


def _swap_halves64(y, lane):
    return jnp.where(lane < 32, pltpu.roll(y, 96, 1), pltpu.roll(y, 32, 1))


def _rmsnorm_body(x_ref, g_ref, o_ref):
    x = x_ref[...].astype(F32)
    y = x * lax.rsqrt(jnp.mean(x * x, axis=-1, keepdims=True) + EPS)
    o_ref[...] = (y * g_ref[...]).astype(o_ref.dtype)


def _rmsnorm_rows(x, g, out_dtype):
    t, d = x.shape
    tm = _pick(t, (640, 512, 320, 256, 128, 64, 8))
    return pl.pallas_call(
        _rmsnorm_body,
        grid=(t // tm,),
        in_specs=[pl.BlockSpec((tm, d), lambda i: (i, 0)),
                  pl.BlockSpec((1, d), lambda i: (0, 0))],
        out_specs=pl.BlockSpec((tm, d), lambda i: (i, 0)),
        out_shape=jax.ShapeDtypeStruct((t, d), out_dtype),
        compiler_params=_cp(("parallel",)),
        name="rmsnorm_rows",
    )(x, g.reshape(1, d))


def _mm_body(x_ref, w_ref, o_ref):
    o_ref[...] = _dot(x_ref[...], w_ref[...]).astype(o_ref.dtype)


def _mm_res_body(x_ref, w_ref, r_ref, o_ref):
    o_ref[...] = (r_ref[...] + _dot(x_ref[...], w_ref[...])).astype(o_ref.dtype)


def _matmul(x, w, out_dtype, residual=None, name="matmul"):
    m, k = x.shape
    n = w.shape[1]
    tm = _pick(m, (640, 512, 256, 128, 64, 8))
    tn = _pick(n, (512, 256, 128))
    in_specs = [pl.BlockSpec((tm, k), lambda j, i: (i, 0)),
                pl.BlockSpec((k, tn), lambda j, i: (0, j))]
    args = [x, w]
    body = _mm_body
    if residual is not None:
        in_specs.append(pl.BlockSpec((tm, tn), lambda j, i: (i, j)))
        args.append(residual)
        body = _mm_res_body
    return pl.pallas_call(
        body,
        grid=(n // tn, m // tm),
        in_specs=in_specs,
        out_specs=pl.BlockSpec((tm, tn), lambda j, i: (i, j)),
        out_shape=jax.ShapeDtypeStruct((m, n), out_dtype),
        compiler_params=_cp(("parallel", "parallel")),
        name=name,
    )(*args)


def _proj_multi_body(x_ref, w_ref, *o_refs, widths):
    y = _dot(x_ref[...], w_ref[...])
    off = 0
    for o_ref, wd in zip(o_refs, widths):
        o_ref[...] = y[:, off:off + wd]
        off += wd


def _proj_multi(x, w, widths):
    m, k = x.shape
    n = w.shape[1]
    tm = _pick(m, (640, 512, 256, 128, 64, 8))
    return pl.pallas_call(
        functools.partial(_proj_multi_body, widths=widths),
        grid=(m // tm,),
        in_specs=[pl.BlockSpec((tm, k), lambda i: (i, 0)),
                  pl.BlockSpec((k, n), lambda i: (0, 0))],
        out_specs=[pl.BlockSpec((tm, wd), lambda i: (i, 0)) for wd in widths],
        out_shape=[jax.ShapeDtypeStruct((m, wd), F32) for wd in widths],
        compiler_params=_cp(("parallel",)),
        name="proj_multi",
    )(x, w)


def _conv_finish(acc, o_ref, c, nq_blocks, dk):
    y = acc * jax.nn.sigmoid(acc)
    cw = y.shape[1]

    def normed(scale):
        parts = []
        for s in range(cw // dk):
            seg = y[:, s * dk:(s + 1) * dk]
            parts.append(seg * (lax.rsqrt(jnp.sum(seg * seg, axis=-1, keepdims=True) + EPS) * scale))
        return jnp.concatenate(parts, axis=-1) if len(parts) > 1 else parts[0]

    @pl.when(c < nq_blocks)
    def _():
        o_ref[...] = normed(dk ** -0.5)

    @pl.when(jnp.logical_and(c >= nq_blocks, c < 2 * nq_blocks))
    def _():
        o_ref[...] = normed(1.0)

    @pl.when(c >= 2 * nq_blocks)
    def _():
        o_ref[...] = y


def _conv_prompt_body(x_ref, w_ref, o_ref, *, nq_blocks, dk, conv):
    x = x_ref[...]
    w = w_ref[...]
    row = lax.broadcasted_iota(jnp.int32, x.shape, 0)
    acc = None
    for j in range(conv):
        sh = conv - 1 - j
        xs = x if sh == 0 else jnp.where(row >= sh, pltpu.roll(x, sh, 0), 0.0)
        term = xs * w[j:j + 1, :]
        acc = term if acc is None else acc + term
    _conv_finish(acc, o_ref, pl.program_id(1), nq_blocks, dk)


def _conv_sample_body(*refs, nq_blocks, dk, conv):
    hist = refs[:conv - 1]
    x_ref, w_ref, o_ref = refs[conv - 1:]
    w = w_ref[...]
    acc = None
    for j in range(conv):
        src = x_ref[...] if j == conv - 1 else hist[j][...]
        term = src * w[j:j + 1, :]
        acc = term if acc is None else acc + term
    _conv_finish(acc, o_ref, pl.program_id(0), nq_blocks, dk)


def _gdn_prompt_body(q_ref, k_ref, v_ref, zg_ref, ab_ref, par_ref, gn_ref, o_ref, s_ref, *, hb, chunk, dk):
    seq = q_ref.shape[0]
    n_chunks = seq // chunk
    s_ref[...] = jnp.zeros(s_ref.shape, F32)
    ii = lax.broadcasted_iota(jnp.int32, (chunk, chunk), 0)
    jj = lax.broadcasted_iota(jnp.int32, (chunk, chunk), 1)
    causal = ii >= jj
    strict = ii > jj
    rowi = lax.broadcasted_iota(jnp.int32, (chunk, LANES), 0)
    par = par_ref[0]
    a_log = par[0:1, :]
    dt_bias = par[1:2, :]
    gnorm = gn_ref[...]
    n_double = int(math.log2(chunk)) - 1

    def chunk_step(ci, carry):
        r0 = pl.multiple_of(ci * chunk, chunk)
        ab = ab_ref[pl.ds(r0, chunk), :]
        g = -jnp.exp(a_log) * _softplus(ab + dt_bias)
        beta = jax.nn.sigmoid(ab)
        gc = g
        sh = 1
        while sh < chunk:
            gc = gc + jnp.where(rowi >= sh, pltpu.roll(gc, sh, 0), 0.0)
            sh *= 2
        gct = gc.T
        for hh in range(hb):
            cs = slice(hh * dk, (hh + 1) * dk)
            gcol = gc[:, hh:hh + 1]
            grow = gct[hh:hh + 1, :]
            bcol = beta[:, 64 + hh:65 + hh]
            q = q_ref[pl.ds(r0, chunk), cs]
            k = k_ref[pl.ds(r0, chunk), cs]
            v = v_ref[pl.ds(r0, chunk), cs]
            decay = jnp.exp(jnp.where(causal, gcol - grow, -jnp.inf))
            kb = k * bcol
            kbf = k.astype(BF16)
            x = -jnp.where(strict, _nt(kb.astype(BF16), kbf) * decay, 0.0)
            nmat = x
            p = x
            for _ in range(n_double):
                p = _dot(p, p, precision=lax.Precision.HIGHEST)
                nmat = nmat + p + _dot(nmat, p, precision=lax.Precision.HIGHEST)
            nb = nmat.astype(BF16)
            vb = v * bcol
            u = vb + _dot(nb, vb.astype(BF16))
            kg = kb * jnp.exp(gcol)
            w = kg + _dot(nb, kg.astype(BF16))
            qk = jnp.where(causal, _nt(q.astype(BF16), kbf) * decay, 0.0)
            st = s_ref[0, hh]
            stb = st.astype(BF16)
            v_new = u - _dot(w.astype(BF16), stb)
            vnb = v_new.astype(BF16)
            o = _dot((q * jnp.exp(gcol)).astype(BF16), stb) + _dot(qk.astype(BF16), vnb)
            g_last = gc[chunk - 1:chunk, hh:hh + 1]
            kd = k * jnp.exp(g_last - gcol)
            s_ref[0, hh] = st * jnp.exp(g_last) + _tn(kd.astype(BF16), vnb)
            on = o * lax.rsqrt(jnp.mean(o * o, axis=-1, keepdims=True) + EPS) * gnorm
            zg = zg_ref[pl.ds(r0, chunk), cs]
            o_ref[pl.ds(r0, chunk), cs] = (on * (zg * jax.nn.sigmoid(zg))).astype(o_ref.dtype)
        return carry

    lax.fori_loop(0, n_chunks, chunk_step, 0)


def _gdn_sample_body(qkv_ref, qkt_ref, ab_ref, par_ref, zg_ref, gn_ref, st_ref, o_ref, so_ref, *, gh, hb, dk):
    ab = ab_ref[0]
    par = par_ref[...]
    g = -jnp.exp(par[0:1, :]) * _softplus(ab + par[1:2, :])
    dec = jnp.exp(g)
    beta = jax.nn.sigmoid(ab)
    qkt = qkt_ref[0]
    gnorm = gn_ref[...]
    for h in range(gh):
        la = (h // hb) * LANES + h % hb
        lb = la + 64
        st = st_ref[0, h] * dec[:, la:la + 1]
        kcol = qkt[:, gh + h:gh + h + 1]
        qcol = qkt[:, h:h + 1]
        v = qkv_ref[0, 2 * gh + h:2 * gh + h + 1, :]
        kst = jnp.sum(st * kcol, axis=0, keepdims=True)
        delta = (v - kst) * beta[:, lb:lb + 1]
        st = st + kcol * delta
        so_ref[0, h] = st
        o = jnp.sum(st * qcol, axis=0, keepdims=True)
        on = o * lax.rsqrt(jnp.mean(o * o, axis=-1, keepdims=True) + EPS) * gnorm
        zg = zg_ref[0, :, h * dk:(h + 1) * dk]
        o_ref[0, :, h * dk:(h + 1) * dk] = (on * (zg * jax.nn.sigmoid(zg))).astype(o_ref.dtype)


def _mla_q_body(c_ref, g_ref, w_ref, gq_ref, cos_ref, sin_ref, o_ref, *, mh, qk_dim, scale):
    x = c_ref[...]
    xn = x * lax.rsqrt(jnp.mean(x * x, axis=-1, keepdims=True) + EPS) * g_ref[...]
    q = _dot(xn.astype(BF16), w_ref[...])
    cos = cos_ref[...]
    sin = sin_ref[...]
    lane = lax.broadcasted_iota(jnp.int32, cos.shape, 1)
    gq = gq_ref[...]
    for h in range(mh):
        qn = q[:, h * 256:h * 256 + 128]
        qr = q[:, h * 256 + 128:(h + 1) * 256]
        ss = jnp.sum(qn * qn, axis=-1, keepdims=True) + jnp.sum(qr * qr, axis=-1, keepdims=True)
        rinv = lax.rsqrt(ss * (1.0 / qk_dim) + EPS) * scale
        y = qr * rinv * gq[:, 128:]
        o_ref[:, h * 256:h * 256 + 128] = (qn * rinv * gq[:, :128]).astype(o_ref.dtype)
        o_ref[:, h * 256 + 128:(h + 1) * 256] = (y * cos + _swap_halves64(y, lane) * sin).astype(o_ref.dtype)


def _mla_k_body(c_ref, kr_ref, g_ref, wk_ref, wv_ref, gk_ref, cos_ref, sin_ref, lat_ref, k_ref, v_ref,
                *, mh, qk_dim):
    x = c_ref[...]
    c = x * lax.rsqrt(jnp.mean(x * x, axis=-1, keepdims=True) + EPS) * g_ref[...]
    lat_ref[...] = c
    cb = c.astype(BF16)
    kn = _dot(cb, wk_ref[...])
    v_ref[...] = _dot(cb, wv_ref[...]).astype(v_ref.dtype)
    kr = kr_ref[...]
    ssr = jnp.sum(kr * kr, axis=-1, keepdims=True)
    gk = gk_ref[...]
    cos = cos_ref[...]
    sin = sin_ref[...]
    lane = lax.broadcasted_iota(jnp.int32, cos.shape, 1)
    y = kr * gk[:, 128:]
    krot = y * cos + _swap_halves64(y, lane) * sin
    for h in range(mh):
        knh = kn[:, h * 128:(h + 1) * 128]
        rinv = lax.rsqrt((jnp.sum(knh * knh, axis=-1, keepdims=True) + ssr) * (1.0 / qk_dim) + EPS)
        k_ref[:, h * 256:h * 256 + 128] = (knh * rinv * gk[:, :128]).astype(k_ref.dtype)
        k_ref[:, h * 256 + 128:(h + 1) * 256] = (krot * rinv).astype(k_ref.dtype)


def _flash_body(q_ref, k_ref, v_ref, o_ref, m_scr, l_scr, acc_scr, *, tq, tk):
    i = pl.program_id(2)
    j = pl.program_id(3)

    @pl.when(j == 0)
    def _():
        m_scr[...] = jnp.full(m_scr.shape, -jnp.inf, F32)
        l_scr[...] = jnp.zeros(l_scr.shape, F32)
        acc_scr[...] = jnp.zeros(acc_scr.shape, F32)

    @pl.when(j * tk <= i * tq + tq - 1)
    def _():
        s = _nt(q_ref[...], k_ref[...])
        qpos = i * tq + lax.broadcasted_iota(jnp.int32, s.shape, 0)
        kpos = j * tk + lax.broadcasted_iota(jnp.int32, s.shape, 1)
        s = jnp.where(kpos <= qpos, s, -jnp.inf)
        m_old = m_scr[...]
        m_new = jnp.maximum(m_old, jnp.max(s, axis=-1, keepdims=True))
        corr = jnp.exp(m_old - m_new)
        p = jnp.exp(s - m_new)
        l_scr[...] = l_scr[...] * corr + jnp.sum(p, axis=-1, keepdims=True)
        acc_scr[...] = acc_scr[...] * corr + _dot(p.astype(BF16), v_ref[...])
        m_scr[...] = m_new

    @pl.when(j == pl.num_programs(3) - 1)
    def _():
        o_ref[...] = (acc_scr[...] / l_scr[...]).astype(o_ref.dtype)


def _sample_prep_body(q_ref, k_ref, wukt_ref, gk_ref, qabs_ref, qab_ref, ss_ref, *, mh, kvr):
    gk = gk_ref[...]
    db = q_ref.shape[0]
    lane = lax.broadcasted_iota(jnp.int32, (db, LANES), 1)
    sgn = jnp.where(lane < 32, 1.0, -1.0)
    cols = []
    for h in range(mh):
        qh = q_ref[:, h * 256:(h + 1) * 256].astype(F32)
        kh = k_ref[:, h * 256:(h + 1) * 256].astype(F32)
        cols.append(jnp.sum(qh * kh, axis=-1, keepdims=True))
        qn = qh[:, :128] * gk[:, :128]
        qabs = _dot(qn, wukt_ref[h], precision=lax.Precision.HIGHEST)
        qabs_ref[:, h * kvr:(h + 1) * kvr] = qabs.astype(qabs_ref.dtype)
        qraw = qh[:, 128:]
        qa = qraw * gk[:, 128:]
        qb = _swap_halves64(qraw, lane) * gk[:, 128:] * sgn
        qab_ref[:, h * 128:(h + 1) * 128] = (qa + pltpu.roll(qb, 64, 1)).astype(qab_ref.dtype)
    ss_ref[...] = jnp.concatenate(cols, axis=-1)


def _sample_attn_body(pt_ref, qabs_ref, qab_ref, ss_ref, cnew_ref, wukt_ref, ctab_ref, stab_ref, *rest,
                      npp, mh, nope, qk_dim, page):
    lat_refs = rest[:npp]
    kr_refs = rest[npp:2 * npp]
    o_ref = rest[2 * npp]
    m_scr, l_scr, acc_scr = rest[2 * npp + 1:]
    g = pl.program_id(1)

    @pl.when(g == 0)
    def _():
        m_scr[...] = ss_ref[0]
        l_scr[...] = jnp.ones(l_scr.shape, F32)
        acc_scr[...] = jnp.broadcast_to(cnew_ref[0], acc_scr.shape)

    qabs = qabs_ref[0]
    qab = qab_ref[0]
    qa = qab[:, :64]
    qb = qab[:, 64:]
    wukt = wukt_ref[...]
    ones8 = jnp.ones((8, 64), F32)
    group = 2 if npp % 2 == 0 else 1
    for sub in range(npp // group):
        ids = range(sub * group, (sub + 1) * group)
        c2 = jnp.concatenate([lat_refs[r][...].astype(BF16) for r in ids], axis=0)
        kr2 = jnp.concatenate([kr_refs[r][...] for r in ids], axis=0)
        n = c2.shape[0]
        rows = pl.ds(sub * group * page, n)
        kt = _nt(wukt, c2)
        ss = jnp.sum((kt * kt).reshape(mh, nope, n), axis=1)
        ssr = _nt(ones8, kr2 * kr2, precision=lax.Precision.HIGHEST)[0:1, :]
        rinv = lax.rsqrt((ss + ssr) * (1.0 / qk_dim) + EPS)
        sn = _nt(qabs, c2)
        sr = (_nt(qa, (kr2 * ctab_ref[rows, :]).astype(BF16))
              + _nt(qb, (kr2 * stab_ref[rows, :]).astype(BF16)))
        s = (sn + sr) * rinv
        m_old = m_scr[...]
        m_new = jnp.maximum(m_old, jnp.max(s, axis=-1, keepdims=True))
        corr = jnp.exp(m_old - m_new)
        p = jnp.exp(s - m_new)
        l_scr[...] = l_scr[...] * corr + jnp.sum(p, axis=-1, keepdims=True)
        acc_scr[...] = acc_scr[...] * corr + _dot(p.astype(BF16), c2)
        m_scr[...] = m_new

    @pl.when(g == pl.num_programs(1) - 1)
    def _():
        o_ref[0] = acc_scr[...] / l_scr[...]


def _head_mm_body(x_ref, w_ref, o_ref):
    o_ref[...] = _dot(x_ref[...].astype(BF16), w_ref[0]).astype(o_ref.dtype)


def _merge_body(og_ref, om_ref, wg_ref, wm_ref, ga_ref, gb_ref, o_ref):
    ya = _dot(og_ref[...], wg_ref[...])
    yb = _dot(om_ref[...], wm_ref[...])
    o_ref[...] = (jax.nn.sigmoid(ga_ref[...]) * ya + jax.nn.sigmoid(gb_ref[...]) * yb).astype(o_ref.dtype)


def _peer_select_body(q_ref, sk_ref, s1_ref, s2_ref, st_ref, *, topk):
    q = q_ref[...].astype(BF16)
    sk = sk_ref[0].astype(BF16)
    s1 = _nt(sk[0], q[:, :128])
    s2 = _nt(sk[1], q[:, 128:])

    def top_values(s):
        vals = []
        for _ in range(topk):
            m = jnp.max(s, axis=0, keepdims=True)
            vals.append(m)
            s = jnp.where(s >= m, -jnp.inf, s)
        return vals

    a = top_values(s1)
    b = jnp.concatenate(top_values(s2), axis=0)
    cand = jnp.concatenate([a[r] + b for r in range(topk)], axis=0)
    best = top_values(cand)
    z = jnp.exp(best[0] - best[0])
    for r in range(1, topk):
        z = z + jnp.exp(best[r] - best[0])
    s1_ref[0] = s1
    s2_ref[0] = s2
    ts = s1.shape[1]
    st_ref[0] = jnp.concatenate([best[topk - 1], best[0] + jnp.log(z), jnp.zeros((6, ts), F32)], axis=0)


def _peer_dense_body(x_ref, u_ref, v_ref, s1_ref, s2_ref, st_ref, o_ref, *, ph, te):
    j = pl.program_id(1)

    @pl.when(j == 0)
    def _():
        o_ref[...] = jnp.zeros(o_ref.shape, F32)

    ht = _nt(u_ref[...], x_ref[...])
    act = 0.5 * ht * (1.0 + lax.erf(ht * (2.0 ** -0.5)))
    tm = ht.shape[1]
    na = te // LANES
    parts = []
    for a in range(na):
        i1 = j * na + a
        wt = jnp.zeros((LANES, tm), F32)
        for h in range(ph):
            ssum = s2_ref[h] + s1_ref[h, pl.ds(i1, 1), :]
            wt = wt + jnp.where(ssum >= st_ref[h, 0:1, :], jnp.exp(ssum - st_ref[h, 1:2, :]), 0.0)
        parts.append(wt)
    wt = jnp.concatenate(parts, axis=0) if na > 1 else parts[0]
    p = (wt * act).T.astype(BF16)
    o_ref[...] += _dot(p, v_ref[...])


def _add_body(a_ref, b_ref, o_ref):
    o_ref[...] = a_ref[...] + b_ref[...]


def _add_rows(a, b, row0, nrows, tm):
    d = a.shape[1]
    blk0 = row0 // tm
    return pl.pallas_call(
        _add_body,
        grid=(nrows // tm,),
        in_specs=[pl.BlockSpec((tm, d), lambda i: (i + blk0, 0)),
                  pl.BlockSpec((tm, d), lambda i: (i + blk0, 0))],
        out_specs=pl.BlockSpec((tm, d), lambda i: (i, 0)),
        out_shape=jax.ShapeDtypeStruct((nrows, d), F32),
        compiler_params=_cp(("parallel",)),
        name="residual_add",
    )(a, b)


def _rope_tables(pos, rope):
    half = rope // 2
    inv = jnp.power(ROPE_THETA, -2.0 * jnp.arange(half, dtype=F32) / rope)
    ang = pos.astype(F32)[:, None] * inv[None, :]
    return jnp.cos(ang), jnp.sin(ang)


def kernel(x_prompt, x_sample, cache_latent, cache_krope, page_table, state_gdn, state_conv, norm_mix_g, w_in,
           gdn_conv_w, gdn_a_log, gdn_dt_bias, gdn_norm_g, gdn_out, mla_q_norm_g, mla_kv_norm_g, mla_w_uq,
           mla_w_uk, mla_w_uv, mla_qk_norm_q, mla_qk_norm_k, mla_out, w_o, norm_ffn_g, peer_w_q, peer_subkeys,
           peer_u, peer_v):
    depth = w_in.shape[0]
    assert depth == 1, "single-layer trunk"
    bsz, seq, d = x_prompt.shape
    db, dseq, _ = x_sample.shape
    assert dseq == 1
    tp = bsz * seq
    t = tp + db
    page = cache_latent.shape[2]
    kvr = cache_latent.shape[3]
    rope = cache_krope.shape[3]
    n_pages = page_table.shape[1]
    past_len = n_pages * page
    gh, dk, dv = state_gdn.shape[2:]
    conv = gdn_conv_w.shape[1]
    cw = gdn_conv_w.shape[2]
    qkw = gh * dk
    vw = gh * dv
    qr = mla_q_norm_g.shape[1]
    mh, nope = mla_w_uk.shape[2:]
    vdim = mla_w_uv.shape[3]
    qk_dim = nope + rope
    ph, _, nkeys, qhalf = peer_subkeys.shape[1:]
    n_exp = peer_u.shape[1]
    assert dk == LANES and dv == LANES and nope == LANES and vdim == LANES and rope == 64
    assert nkeys == LANES and qhalf == LANES and page == LANES and n_exp == nkeys * nkeys
    assert tp % db == 0 and seq % GDN_CHUNK == 0

    hb = _pick(gh, (4, 2, 1))
    ngrp = gh // hb
    assert hb <= 64

    sizes = (cw, gh, gh, vw, qr, kvr, rope, d, d)
    offs = [0]
    for s in sizes:
        offs.append(offs[-1] + s)
    w_in0 = w_in[0]

    def wcol(i):
        return w_in0[:, offs[i]:offs[i + 1]]

    w_big = jnp.concatenate([wcol(0), wcol(3), wcol(7), wcol(8)], axis=1).astype(BF16)
    off_zg, off_ga, off_gb = cw, cw + vw, cw + vw + d
    wa = wcol(1).reshape(d, ngrp, hb)
    wb = wcol(2).reshape(d, ngrp, hb)
    w_ab = jnp.concatenate([wa, jnp.zeros((d, ngrp, 64 - hb), F32), wb, jnp.zeros((d, ngrp, 64 - hb), F32)],
                           axis=2).reshape(d, ngrp * LANES)
    w_small = jnp.concatenate([wcol(4), wcol(5), wcol(6), jnp.zeros((d, LANES - rope), F32), w_ab],
                              axis=1).astype(BF16)

    def group_lanes(vec):
        return jnp.pad(vec.reshape(ngrp, hb), ((0, 0), (0, LANES - hb))).reshape(ngrp * LANES)

    par = jnp.stack([group_lanes(gdn_a_log[0]), group_lanes(gdn_dt_bias[0])], axis=0)

    x_all = jnp.concatenate([x_prompt.reshape(tp, d), x_sample.reshape(db, d)], axis=0)
    xn = _rmsnorm_rows(x_all, norm_mix_g[0], BF16)
    z_big = _matmul(xn, w_big, F32, name="in_proj_big")
    c_q, c_kv_raw, kr_pad, ab = _proj_multi(xn, w_small, (qr, kvr, LANES, ngrp * LANES))

    cwb = _pick(qkw, (512, 256, 128))
    nq_blocks = qkw // cwb
    conv_w = gdn_conv_w[0]
    qkv_act_p = pl.pallas_call(
        functools.partial(_conv_prompt_body, nq_blocks=nq_blocks, dk=dk, conv=conv),
        grid=(bsz, cw // cwb),
        in_specs=[pl.BlockSpec((seq, cwb), lambda b, c: (b, c)),
                  pl.BlockSpec((conv, cwb), lambda b, c: (0, c))],
        out_specs=pl.BlockSpec((seq, cwb), lambda b, c: (b, c)),
        out_shape=jax.ShapeDtypeStruct((tp, cw), F32),
        compiler_params=_cp(("parallel", "parallel")),
        name="gdn_conv_prompt",
    )(z_big, conv_w)

    st_conv2 = state_conv[0].reshape(db, (conv - 1) * cw)
    ncb = cw // cwb
    srow = tp // db
    qkv_act_s = pl.pallas_call(
        functools.partial(_conv_sample_body, nq_blocks=nq_blocks, dk=dk, conv=conv),
        grid=(ncb,),
        in_specs=[pl.BlockSpec((db, cwb), functools.partial(lambda c, j: (0, j * ncb + c), j=j))
                  for j in range(conv - 1)]
                 + [pl.BlockSpec((db, cwb), lambda c: (srow, c)),
                    pl.BlockSpec((conv, cwb), lambda c: (0, c))],
        out_specs=pl.BlockSpec((db, cwb), lambda c: (0, c)),
        out_shape=jax.ShapeDtypeStruct((db, cw), F32),
        compiler_params=_cp(("parallel",)),
        name="gdn_conv_sample",
    )(*([st_conv2] * (conv - 1)), z_big, conv_w)

    hw = hb * dk
    assert off_zg % hw == 0 and qkw % hw == 0
    par3 = par.reshape(2, ngrp, LANES).transpose(1, 0, 2)
    gnorm = gdn_norm_g[0].reshape(1, dv)
    o_gdn_p, s_gdn_p = pl.pallas_call(
        functools.partial(_gdn_prompt_body, hb=hb, chunk=GDN_CHUNK, dk=dk),
        grid=(bsz, ngrp),
        in_specs=[pl.BlockSpec((seq, hw), lambda b, g: (b, g)),
                  pl.BlockSpec((seq, hw), lambda b, g: (b, ngrp + g)),
                  pl.BlockSpec((seq, hw), lambda b, g: (b, 2 * ngrp + g)),
                  pl.BlockSpec((seq, hw), lambda b, g: (b, off_zg // hw + g)),
                  pl.BlockSpec((seq, LANES), lambda b, g: (b, g)),
                  pl.BlockSpec((1, 2, LANES), lambda b, g: (g, 0, 0)),
                  pl.BlockSpec((1, dv), lambda b, g: (0, 0))],
        out_specs=[pl.BlockSpec((seq, hw), lambda b, g: (b, g)),
                   pl.BlockSpec((1, hb, dk, dv), lambda b, g: (b, g, 0, 0))],
        out_shape=[jax.ShapeDtypeStruct((tp, vw), BF16),
                   jax.ShapeDtypeStruct((bsz, gh, dk, dv), F32)],
        compiler_params=_cp(("parallel", "parallel")),
        name="gdn_prompt",
    )(qkv_act_p, qkv_act_p, qkv_act_p, z_big, ab, par3, gnorm)

    qkv_s3 = qkv_act_s.reshape(db, 3 * gh, dk)
    qkt_s = jnp.swapaxes(qkv_s3[:, :2 * gh, :], 1, 2)
    ab_s = ab[tp:].reshape(db, 1, ngrp * LANES)
    zg_s = z_big[tp:, off_zg:off_zg + vw].reshape(db, 1, vw)
    o_gdn_s, s_gdn_s = pl.pallas_call(
        functools.partial(_gdn_sample_body, gh=gh, hb=hb, dk=dk),
        grid=(db,),
        in_specs=[pl.BlockSpec((1, 3 * gh, dk), lambda b: (b, 0, 0)),
                  pl.BlockSpec((1, dk, 2 * gh), lambda b: (b, 0, 0)),
                  pl.BlockSpec((1, 1, ngrp * LANES), lambda b: (b, 0, 0)),
                  pl.BlockSpec((2, ngrp * LANES), lambda b: (0, 0)),
                  pl.BlockSpec((1, 1, vw), lambda b: (b, 0, 0)),
                  pl.BlockSpec((1, dv), lambda b: (0, 0)),
                  pl.BlockSpec((1, gh, dk, dv), lambda b: (b, 0, 0, 0))],
        out_specs=[pl.BlockSpec((1, 1, vw), lambda b: (b, 0, 0)),
                   pl.BlockSpec((1, gh, dk, dv), lambda b: (b, 0, 0, 0))],
        out_shape=[jax.ShapeDtypeStruct((db, 1, vw), BF16),
                   jax.ShapeDtypeStruct((db, gh, dk, dv), F32)],
        compiler_params=_cp(("parallel",)),
        name="gdn_sample",
    )(qkv_s3, qkt_s, ab_s, par, zg_s, gnorm, state_gdn[0].astype(F32))
    o_gdn = jnp.concatenate([o_gdn_p, o_gdn_s.reshape(db, vw)], axis=0)

    pos_all = jnp.concatenate([jnp.tile(jnp.arange(seq, dtype=jnp.int32), bsz),
                               jnp.full((db,), past_len, jnp.int32)])
    cos_h, sin_h = _rope_tables(pos_all, rope)
    zpad = jnp.zeros((t, LANES - rope), F32)
    cos_t = jnp.concatenate([cos_h, cos_h, zpad], axis=1)
    sin_t = jnp.concatenate([-sin_h, sin_h, zpad], axis=1)

    w_uq3 = mla_w_uq[0].reshape(qr, mh, qk_dim)
    w_uq_pad = jnp.concatenate([w_uq3, jnp.zeros((qr, mh, 256 - qk_dim), F32)], axis=2).reshape(qr, mh * 256)
    w_uq_pad = w_uq_pad.astype(BF16)

    def gain256(gvec):
        return jnp.concatenate([gvec, jnp.zeros((256 - qk_dim,), F32)]).reshape(1, 256)

    tmq = _pick(t, (320, 256, 128, 64, 8))
    sm_scale = qk_dim ** -0.5
    q_all = pl.pallas_call(
        functools.partial(_mla_q_body, mh=mh, qk_dim=qk_dim, scale=sm_scale),
        grid=(t // tmq,),
        in_specs=[pl.BlockSpec((tmq, qr), lambda i: (i, 0)),
                  pl.BlockSpec((1, qr), lambda i: (0, 0)),
                  pl.BlockSpec((qr, mh * 256), lambda i: (0, 0)),
                  pl.BlockSpec((1, 256), lambda i: (0, 0)),
                  pl.BlockSpec((tmq, LANES), lambda i: (i, 0)),
                  pl.BlockSpec((tmq, LANES), lambda i: (i, 0))],
        out_specs=pl.BlockSpec((tmq, mh * 256), lambda i: (i, 0)),
        out_shape=jax.ShapeDtypeStruct((t, mh * 256), BF16),
        compiler_params=_cp(("parallel",)),
        name="mla_q",
    )(c_q, mla_q_norm_g[0].reshape(1, qr), w_uq_pad, gain256(mla_qk_norm_q[0]), cos_t, sin_t)

    w_uk2 = mla_w_uk[0].reshape(kvr, mh * nope).astype(BF16)
    w_uv2 = mla_w_uv[0].reshape(kvr, mh * vdim).astype(BF16)
    gk256 = gain256(mla_qk_norm_k[0])
    lat_all, k_all, v_all = pl.pallas_call(
        functools.partial(_mla_k_body, mh=mh, qk_dim=qk_dim),
        grid=(t // tmq,),
        in_specs=[pl.BlockSpec((tmq, kvr), lambda i: (i, 0)),
                  pl.BlockSpec((tmq, LANES), lambda i: (i, 0)),
                  pl.BlockSpec((1, kvr), lambda i: (0, 0)),
                  pl.BlockSpec((kvr, mh * nope), lambda i: (0, 0)),
                  pl.BlockSpec((kvr, mh * vdim), lambda i: (0, 0)),
                  pl.BlockSpec((1, 256), lambda i: (0, 0)),
                  pl.BlockSpec((tmq, LANES), lambda i: (i, 0)),
                  pl.BlockSpec((tmq, LANES), lambda i: (i, 0))],
        out_specs=[pl.BlockSpec((tmq, kvr), lambda i: (i, 0)),
                   pl.BlockSpec((tmq, mh * 256), lambda i: (i, 0)),
                   pl.BlockSpec((tmq, mh * vdim), lambda i: (i, 0))],
        out_shape=[jax.ShapeDtypeStruct((t, kvr), F32),
                   jax.ShapeDtypeStruct((t, mh * 256), BF16),
                   jax.ShapeDtypeStruct((t, mh * vdim), BF16)],
        compiler_params=_cp(("parallel",)),
        name="mla_k",
    )(c_kv_raw, kr_pad, mla_kv_norm_g[0].reshape(1, kvr), w_uk2, w_uv2, gk256, cos_t, sin_t)

    tq = _pick(seq, (512, 256, 128))
    tk = tq
    nqb = seq // tq
    o_mla_p = pl.pallas_call(
        functools.partial(_flash_body, tq=tq, tk=tk),
        grid=(bsz, mh, nqb, nqb),
        in_specs=[pl.BlockSpec((tq, 256), lambda b, h, i, j: (b * nqb + i, h)),
                  pl.BlockSpec((tk, 256), lambda b, h, i, j: (b * nqb + jnp.minimum(j, i), h)),
                  pl.BlockSpec((tk, vdim), lambda b, h, i, j: (b * nqb + jnp.minimum(j, i), h))],
        out_specs=pl.BlockSpec((tq, vdim), lambda b, h, i, j: (b * nqb + i, h)),
        out_shape=jax.ShapeDtypeStruct((tp, mh * vdim), BF16),
        scratch_shapes=[pltpu.VMEM((tq, 1), F32), pltpu.VMEM((tq, 1), F32), pltpu.VMEM((tq, vdim), F32)],
        compiler_params=_cp(("parallel", "parallel", "parallel", "arbitrary")),
        name="mla_flash_prompt",
    )(q_all, k_all, v_all)

    w_ukt = jnp.transpose(mla_w_uk[0], (1, 2, 0))
    q_s = q_all[tp:]
    k_s = k_all[tp:]
    qabs, qab, s_self = pl.pallas_call(
        functools.partial(_sample_prep_body, mh=mh, kvr=kvr),
        out_shape=[jax.ShapeDtypeStruct((db, mh * kvr), BF16),
                   jax.ShapeDtypeStruct((db, mh * LANES), BF16),
                   jax.ShapeDtypeStruct((db, mh), F32)],
        compiler_params=pltpu.CompilerParams(vmem_limit_bytes=VMEM_LIMIT),
        name="mla_sample_prep",
    )(q_s, k_s, w_ukt, gk256)

    npp = _pick(n_pages, (8, 4, 2, 1))
    pos_past = jnp.arange(past_len, dtype=jnp.int32)
    cos_p, sin_p = _rope_tables(pos_past, rope)
    ctab = jnp.concatenate([cos_p, cos_p], axis=1)
    stab = jnp.concatenate([sin_p, sin_p], axis=1)
    w_ukt2 = w_ukt.reshape(mh * nope, kvr).astype(BF16)
    lat0 = cache_latent[0]
    kr0 = cache_krope[0]

    def page_map(r):
        return lambda b, g, pt: (pt[b * n_pages + g * npp + r], 0, 0)

    lat_s = pl.pallas_call(
        functools.partial(_sample_attn_body, npp=npp, mh=mh, nope=nope, qk_dim=qk_dim, page=page),
        grid_spec=pltpu.PrefetchScalarGridSpec(
            num_scalar_prefetch=1,
            grid=(db, n_pages // npp),
            in_specs=[pl.BlockSpec((1, mh, kvr), lambda b, g, pt: (b, 0, 0)),
                      pl.BlockSpec((1, mh, LANES), lambda b, g, pt: (b, 0, 0)),
                      pl.BlockSpec((1, mh, 1), lambda b, g, pt: (b, 0, 0)),
                      pl.BlockSpec((1, 1, kvr), lambda b, g, pt: (b, 0, 0)),
                      pl.BlockSpec((mh * nope, kvr), lambda b, g, pt: (0, 0)),
                      pl.BlockSpec((npp * page, rope), lambda b, g, pt: (g, 0)),
                      pl.BlockSpec((npp * page, rope), lambda b, g, pt: (g, 0))]
                     + [pl.BlockSpec((None, page, kvr), page_map(r)) for r in range(npp)]
                     + [pl.BlockSpec((None, page, rope), page_map(r)) for r in range(npp)],
            out_specs=pl.BlockSpec((1, mh, kvr), lambda b, g, pt: (b, 0, 0)),
            scratch_shapes=[pltpu.VMEM((mh, 1), F32), pltpu.VMEM((mh, 1), F32), pltpu.VMEM((mh, kvr), F32)],
        ),
        out_shape=jax.ShapeDtypeStruct((db, mh, kvr), F32),
        compiler_params=_cp(("parallel", "arbitrary")),
        name="mla_sample_attn",
    )(page_table.reshape(-1).astype(jnp.int32),
      qabs.reshape(db, mh, kvr), qab.reshape(db, mh, LANES), s_self.reshape(db, mh, 1),
      lat_all[tp:].reshape(db, 1, kvr), w_ukt2, ctab, stab,
      *([lat0] * npp), *([kr0] * npp))

    w_uv3 = jnp.transpose(mla_w_uv[0], (1, 0, 2)).astype(BF16)
    o_mla_s = pl.pallas_call(
        _head_mm_body,
        grid=(mh,),
        in_specs=[pl.BlockSpec((db, kvr), lambda h: (0, h)),
                  pl.BlockSpec((1, kvr, vdim), lambda h: (h, 0, 0))],
        out_specs=pl.BlockSpec((db, vdim), lambda h: (0, h)),
        out_shape=jax.ShapeDtypeStruct((db, mh * vdim), BF16),
        compiler_params=_cp(("parallel",)),
        name="mla_sample_uv",
    )(lat_s.reshape(db, mh * kvr), w_uv3)
    o_mla = jnp.concatenate([o_mla_p, o_mla_s], axis=0)

    tm = _pick(t, (640, 512, 256, 128, 64, 8))
    tn = _pick(d, (512, 256, 128))
    assert off_ga % tn == 0 and off_gb % tn == 0
    merged = pl.pallas_call(
        _merge_body,
        grid=(d // tn, t // tm),
        in_specs=[pl.BlockSpec((tm, vw), lambda j, i: (i, 0)),
                  pl.BlockSpec((tm, mh * vdim), lambda j, i: (i, 0)),
                  pl.BlockSpec((vw, tn), lambda j, i: (0, j)),
                  pl.BlockSpec((mh * vdim, tn), lambda j, i: (0, j)),
                  pl.BlockSpec((tm, tn), lambda j, i: (i, off_ga // tn + j)),
                  pl.BlockSpec((tm, tn), lambda j, i: (i, off_gb // tn + j))],
        out_specs=pl.BlockSpec((tm, tn), lambda j, i: (i, j)),
        out_shape=jax.ShapeDtypeStruct((t, d), BF16),
        compiler_params=_cp(("parallel", "parallel")),
        name="mixer_merge",
    )(o_gdn, o_mla, gdn_out[0].astype(BF16), mla_out[0].astype(BF16), z_big, z_big)
    x2 = _matmul(merged, w_o[0].astype(BF16), F32, residual=x_all, name="out_proj")

    hf = _rmsnorm_rows(x2, norm_ffn_g[0], BF16)
    pq = _matmul(hf, peer_w_q[0].astype(BF16), F32, name="peer_query")
    ts = _pick(t, (128,))
    s1t, s2t, stat = pl.pallas_call(
        functools.partial(_peer_select_body, topk=PEER_TOPK),
        grid=(t // ts, ph),
        in_specs=[pl.BlockSpec((ts, 2 * qhalf), lambda i, h: (i, h)),
                  pl.BlockSpec((1, 2, nkeys, qhalf), lambda i, h: (h, 0, 0, 0))],
        out_specs=[pl.BlockSpec((1, nkeys, ts), lambda i, h: (h, 0, i)),
                   pl.BlockSpec((1, nkeys, ts), lambda i, h: (h, 0, i)),
                   pl.BlockSpec((1, 8, ts), lambda i, h: (h, 0, i))],
        out_shape=[jax.ShapeDtypeStruct((ph, nkeys, t), F32),
                   jax.ShapeDtypeStruct((ph, nkeys, t), F32),
                   jax.ShapeDtypeStruct((ph, 8, t), F32)],
        compiler_params=_cp(("parallel", "parallel")),
        name="peer_select",
    )(pq, peer_subkeys[0])

    tmp = _pick(t, (640, 512, 384, 256, 128))
    te = _pick(n_exp, (256, 128))
    peer_out = pl.pallas_call(
        functools.partial(_peer_dense_body, ph=ph, te=te),
        grid=(t // tmp, n_exp // te),
        in_specs=[pl.BlockSpec((tmp, d), lambda i, j: (i, 0)),
                  pl.BlockSpec((te, d), lambda i, j: (j, 0)),
                  pl.BlockSpec((te, d), lambda i, j: (j, 0)),
                  pl.BlockSpec((ph, nkeys, tmp), lambda i, j: (0, 0, i)),
                  pl.BlockSpec((ph, nkeys, tmp), lambda i, j: (0, 0, i)),
                  pl.BlockSpec((ph, 8, tmp), lambda i, j: (0, 0, i))],
        out_specs=pl.BlockSpec((tmp, d), lambda i, j: (i, 0)),
        out_shape=jax.ShapeDtypeStruct((t, d), F32),
        compiler_params=_cp(("parallel", "arbitrary")),
        name="peer_dense",
    )(hf, peer_u[0].astype(BF16), peer_v[0].astype(BF16), s1t, s2t, stat)

    tma = _pick(db, (128, 64, 8))
    assert tp % tma == 0
    y_p = _add_rows(x2, peer_out, 0, tp, tma).reshape(bsz, seq, d)
    y_s = _add_rows(x2, peer_out, tp, db, tma).reshape(db, 1, d)

    lat_p = lat_all[:tp].reshape(1, bsz, seq, kvr)
    lat_sn = lat_all[tp:].reshape(1, db, 1, kvr)
    kr_p = kr_pad[:tp, :rope].reshape(1, bsz, seq, rope)
    kr_s = kr_pad[tp:, :rope].reshape(1, db, 1, rope)
    qkv_p = z_big[:tp, :cw].reshape(bsz, seq, cw)
    cv_p = qkv_p[:, seq - (conv - 1):, :][None]
    cv_s = jnp.concatenate([state_conv[0][:, 1:, :], z_big[tp:, :cw].reshape(db, 1, cw)], axis=1)[None]
    return (y_p, y_s, lat_p, kr_p, lat_sn, kr_s,
            s_gdn_p[None].astype(state_gdn.dtype), s_gdn_s[None].astype(state_gdn.dtype), cv_p, cv_s)
```

```python
import functools
import math

import jax
import jax.numpy as jnp
from jax import lax
from jax.experimental import pallas as pl
from jax.experimental.pallas import tpu as pltpu

F32 = jnp.float32
BF16 = jnp.bfloat16

EPS = 1e-6
ROPE_THETA = 10000.0
GDN_CHUNK = 64
PEER_TOPK = 16
LANES = 128
VMEM_LIMIT = 56 * 1024 * 1024


def _cp(dims, vmem=VMEM_LIMIT):
    return pltpu.CompilerParams(dimension_semantics=dims, vmem_limit_bytes=vmem)


def _pick(n, cands):
    for c in cands:
        if c <= n and n % c == 0:
            return c
    return n


def _nt(a, b, precision=None):
    return lax.dot_general(a, b, (((1,), (1,)), ((), ())),
                           preferred_element_type=F32, precision=precision)


def _tn(a, b):
    return lax.dot_general(a, b, (((0,), (0,)), ((), ())), preferred_element_type=F32)


def _dot(a, b, precision=None):
    return jnp.dot(a, b, preferred_element_type=F32, precision=precision)


def _softplus(x):
    return jnp.maximum(x, 0.0) + jnp.log1p(jnp.exp(-jnp.abs(x)))


def _swap_halves64(y, lane):
    return jnp.where(lane < 32, pltpu.roll(y, 96, 1), pltpu.roll(y, 32, 1))


def _rmsnorm2_body(xp_ref, xs_ref, g_ref, *rest, np_blocks, with_residual):
    if with_residual:
        y_ref, x2_ref, o_ref = rest
    else:
        (o_ref,) = rest
    i = pl.program_id(0)

    def emit(x):
        if with_residual:
            x = x + y_ref[...]
            x2_ref[...] = x
        y = x * lax.rsqrt(jnp.mean(x * x, axis=-1, keepdims=True) + EPS)
        o_ref[...] = (y * g_ref[...]).astype(o_ref.dtype)

    @pl.when(i < np_blocks)
    def _():
        emit(xp_ref[...])

    @pl.when(i >= np_blocks)
    def _():
        emit(xs_ref[...])


def _rmsnorm_rows2(xp, xs, g, branch=None):
    tp, d = xp.shape
    db = xs.shape[0]
    tm = _pick(db, (128, 64, 8))
    assert tp % tm == 0 and db % tm == 0
    npb = tp // tm
    t = tp + db
    row = pl.BlockSpec((tm, d), lambda i: (i, 0))
    in_specs = [pl.BlockSpec((tm, d), lambda i: (jnp.minimum(i, npb - 1), 0)),
                pl.BlockSpec((tm, d), lambda i: (jnp.maximum(i - npb, 0), 0)),
                pl.BlockSpec((1, d), lambda i: (0, 0))]
    args = [xp, xs, g.reshape(1, d)]
    out_specs = [row]
    out_shape = [jax.ShapeDtypeStruct((t, d), BF16)]
    if branch is not None:
        in_specs.append(row)
        args.append(branch)
        out_specs = [row, row]
        out_shape = [jax.ShapeDtypeStruct((t, d), F32)] + out_shape
    return pl.pallas_call(
        functools.partial(_rmsnorm2_body, np_blocks=npb, with_residual=branch is not None),
        grid=(t // tm,),
        in_specs=in_specs,
        out_specs=out_specs,
        out_shape=out_shape,
        compiler_params=_cp(("parallel",)),
        name="rmsnorm_rows",
    )(*args)


def _mm_body(x_ref, w_ref, o_ref):
    o_ref[...] = _dot(x_ref[...], w_ref[...]).astype(o_ref.dtype)


def _matmul(x, w, out_dtype, name="matmul"):
    m, k = x.shape
    n = w.shape[1]
    tm = _pick(m, (640, 512, 256, 128, 64, 8))
    tn = _pick(n, (512, 256, 128))
    return pl.pallas_call(
        _mm_body,
        grid=(n // tn, m // tm),
        in_specs=[pl.BlockSpec((tm, k), lambda j, i: (i, 0)),
                  pl.BlockSpec((k, tn), lambda j, i: (0, j))],
        out_specs=pl.BlockSpec((tm, tn), lambda j, i: (i, j)),
        out_shape=jax.ShapeDtypeStruct((m, n), out_dtype),
        compiler_params=_cp(("parallel", "parallel")),
        name=name,
    )(x, w)


def _proj_multi_body(x_ref, w_ref, *o_refs, widths):
    y = _dot(x_ref[...], w_ref[...])
    off = 0
    for o_ref, wd in zip(o_refs, widths):
        o_ref[...] = y[:, off:off + wd]
        off += wd


def _proj_multi(x, w, widths):
    m, k = x.shape
    n = w.shape[1]
    tm = _pick(m, (640, 512, 256, 128, 64, 8))
    return pl.pallas_call(
        functools.partial(_proj_multi_body, widths=widths),
        grid=(m // tm,),
        in_specs=[pl.BlockSpec((tm, k), lambda i: (i, 0)),
                  pl.BlockSpec((k, n), lambda i: (0, 0))],
        out_specs=[pl.BlockSpec((tm, wd), lambda i: (i, 0)) for wd in widths],
        out_shape=[jax.ShapeDtypeStruct((m, wd), F32) for wd in widths],
        compiler_params=_cp(("parallel",)),
        name="proj_multi",
    )(x, w)


def _conv_finish(acc, o_ref, c, nq_blocks, dk):
    y = acc * jax.nn.sigmoid(acc)
    cw = y.shape[1]

    def normed(scale):
        parts = []
        for s in range(cw // dk):
            seg = y[:, s * dk:(s + 1) * dk]
            parts.append(seg * (lax.rsqrt(jnp.sum(seg * seg, axis=-1, keepdims=True) + EPS) * scale))
        return jnp.concatenate(parts, axis=-1) if len(parts) > 1 else parts[0]

    @pl.when(c < nq_blocks)
    def _():
        o_ref[...] = normed(dk ** -0.5).astype(o_ref.dtype)

    @pl.when(jnp.logical_and(c >= nq_blocks, c < 2 * nq_blocks))
    def _():
        o_ref[...] = normed(1.0).astype(o_ref.dtype)

    @pl.when(c >= 2 * nq_blocks)
    def _():
        o_ref[...] = y.astype(o_ref.dtype)


def _conv_prompt_body(x_ref, w_ref, o_ref, tail_ref, *, nq_blocks, dk, conv):
    x = x_ref[...]
    w = w_ref[...]
    tail_ref[0] = x[x.shape[0] - (conv - 1):, :]
    row = lax.broadcasted_iota(jnp.int32, x.shape, 0)
    acc = None
    for j in range(conv):
        sh = conv - 1 - j
        xs = x if sh == 0 else jnp.where(row >= sh, pltpu.roll(x, sh, 0), 0.0)
        term = xs * w[j:j + 1, :]
        acc = term if acc is None else acc + term
    _conv_finish(acc, o_ref, pl.program_id(1), nq_blocks, dk)


def _conv_sample_body(*refs, nq_blocks, dk, conv):
    hist = refs[:conv - 1]
    x_ref, w_ref, o_ref = refs[conv - 1:]
    w = w_ref[...]
    acc = None
    for j in range(conv):
        src = x_ref[...] if j == conv - 1 else hist[j][...]
        term = src * w[j:j + 1, :]
        acc = term if acc is None else acc + term
    _conv_finish(acc, o_ref, pl.program_id(0), nq_blocks, dk)


def _gdn_prompt_body(q_ref, k_ref, v_ref, zg_ref, ab_ref, par_ref, gn_ref, o_ref, s_ref,
                     u_scr, w_scr, qg_scr, kd_scr, qk_scr, dl_scr, *, hb, chunk, dk):
    seq = q_ref.shape[0]
    n_chunks = seq // chunk
    s_ref[...] = jnp.zeros(s_ref.shape, F32)
    ii = lax.broadcasted_iota(jnp.int32, (chunk, chunk), 0)
    jj = lax.broadcasted_iota(jnp.int32, (chunk, chunk), 1)
    causal = ii >= jj
    strict = ii > jj
    rowi = lax.broadcasted_iota(jnp.int32, (chunk, LANES), 0)
    par = par_ref[0]
    a_log = par[0:1, :]
    dt_bias = par[1:2, :]
    gnorm = gn_ref[...]
    n_double = int(math.log2(chunk)) - 1

    def prep_step(ci, carry):
        r0 = pl.multiple_of(ci * chunk, chunk)
        rows = pl.ds(r0, chunk)
        ab = ab_ref[rows, :]
        g = -jnp.exp(a_log) * _softplus(ab + dt_bias)
        beta = jax.nn.sigmoid(ab)
        gc = g
        sh = 1
        while sh < chunk:
            gc = gc + jnp.where(rowi >= sh, pltpu.roll(gc, sh, 0), 0.0)
            sh *= 2
        gct = gc.T
        dl_scr[pl.ds(ci, 1), :] = jnp.exp(gc[chunk - 1:chunk, :])
        heads = range(hb)
        xs, kbs, vbs, gcols = [], [], [], []
        for hh in heads:
            cs = slice(hh * dk, (hh + 1) * dk)
            gcol = gc[:, hh:hh + 1]
            bcol = beta[:, 64 + hh:65 + hh]
            q = q_ref[rows, cs].astype(F32)
            k = k_ref[rows, cs].astype(F32)
            kbf = k_ref[rows, cs].astype(BF16)
            decay = jnp.exp(jnp.where(causal, gcol - gct[hh:hh + 1, :], -jnp.inf))
            kb = k * bcol
            xs.append(-jnp.where(strict, _nt(kb.astype(BF16), kbf) * decay, 0.0))
            qk = jnp.where(causal, _nt(q_ref[rows, cs].astype(BF16), kbf) * decay, 0.0)
            qk_scr[hh, rows, :] = qk.astype(BF16)
            g_last = gc[chunk - 1:chunk, hh:hh + 1]
            qg_scr[rows, cs] = (q * jnp.exp(gcol)).astype(BF16)
            kd_scr[rows, cs] = (k * jnp.exp(g_last - gcol)).astype(BF16)
            kbs.append(kb)
            vbs.append(v_ref[rows, cs].astype(F32) * bcol)
            gcols.append(gcol)
        ns = list(xs)
        ps = list(xs)
        for _ in range(n_double):
            pb = [p.astype(BF16) for p in ps]
            ps = [_dot(b_, b_) for b_ in pb]
            ns = [n_ + p_ + _dot(n_.astype(BF16), p_.astype(BF16)) for n_, p_ in zip(ns, ps)]
        for hh in heads:
            cs = slice(hh * dk, (hh + 1) * dk)
            nb = ns[hh].astype(BF16)
            u_scr[rows, cs] = vbs[hh] + _dot(nb, vbs[hh].astype(BF16))
            kg = kbs[hh] * jnp.exp(gcols[hh])
            w_scr[rows, cs] = (kg + _dot(nb, kg.astype(BF16))).astype(BF16)
        return carry

    lax.fori_loop(0, n_chunks, prep_step, 0)

    def state_step(ci, carry):
        r0 = pl.multiple_of(ci * chunk, chunk)
        rows = pl.ds(r0, chunk)
        dl = dl_scr[pl.ds(ci, 1), :]
        for hh in range(hb):
            cs = slice(hh * dk, (hh + 1) * dk)
            st = s_ref[0, hh]
            stb = st.astype(BF16)
            vnb = (u_scr[rows, cs] - _dot(w_scr[rows, cs], stb)).astype(BF16)
            o = _dot(qg_scr[rows, cs], stb) + _dot(qk_scr[hh, rows, :], vnb)
            s_ref[0, hh] = st * dl[:, hh:hh + 1] + _tn(kd_scr[rows, cs], vnb)
            on = o * lax.rsqrt(jnp.mean(o * o, axis=-1, keepdims=True) + EPS) * gnorm
            zg = zg_ref[rows, cs]
            o_ref[rows, cs] = (on * (zg * jax.nn.sigmoid(zg))).astype(o_ref.dtype)
        return carry

    lax.fori_loop(0, n_chunks, state_step, 0)


def _gdn_sample_body(qkv_ref, qkt_ref, ab_ref, par_ref, zg_ref, gn_ref, st_ref, o_ref, so_ref, *, gh, hb, dk):
    ab = ab_ref[0]
    par = par_ref[...]
    g = -jnp.exp(par[0:1, :]) * _softplus(ab + par[1:2, :])
    dec = jnp.exp(g)
    beta = jax.nn.sigmoid(ab)
    qkt = qkt_ref[0]
    gnorm = gn_ref[...]
    for h in range(gh):
        la = (h // hb) * LANES + h % hb
        lb = la + 64
        st = st_ref[0, h] * dec[:, la:la + 1]
        kcol = qkt[:, gh + h:gh + h + 1]
        qcol = qkt[:, h:h + 1]
        v = qkv_ref[0, 2 * gh + h:2 * gh + h + 1, :]
        kst = jnp.sum(st * kcol, axis=0, keepdims=True)
        delta = (v - kst) * beta[:, lb:lb + 1]
        st = st + kcol * delta
        so_ref[0, h] = st
        o = jnp.sum(st * qcol, axis=0, keepdims=True)
        on = o * lax.rsqrt(jnp.mean(o * o, axis=-1, keepdims=True) + EPS) * gnorm
        zg = zg_ref[0, :, h * dk:(h + 1) * dk]
        o_ref[0, :, h * dk:(h + 1) * dk] = (on * (zg * jax.nn.sigmoid(zg))).astype(o_ref.dtype)


def _mla_q_body(c_ref, g_ref, w_ref, gq_ref, cos_ref, sin_ref, o_ref, *, mh, qk_dim, scale):
    x = c_ref[...]
    xn = x * lax.rsqrt(jnp.mean(x * x, axis=-1, keepdims=True) + EPS) * g_ref[...]
    q = _dot(xn.astype(BF16), w_ref[...])
    cos = cos_ref[...]
    sin = sin_ref[...]
    lane = lax.broadcasted_iota(jnp.int32, cos.shape, 1)
    gq = gq_ref[...]
    for h in range(mh):
        qn = q[:, h * 256:h * 256 + 128]
        qr = q[:, h * 256 + 128:(h + 1) * 256]
        ss = jnp.sum(qn * qn, axis=-1, keepdims=True) + jnp.sum(qr * qr, axis=-1, keepdims=True)
        rinv = lax.rsqrt(ss * (1.0 / qk_dim) + EPS) * scale
        y = qr * rinv * gq[:, 128:]
        o_ref[:, h * 256:h * 256 + 128] = (qn * rinv * gq[:, :128]).astype(o_ref.dtype)
        o_ref[:, h * 256 + 128:(h + 1) * 256] = (y * cos + _swap_halves64(y, lane) * sin).astype(o_ref.dtype)


def _mla_k_body(c_ref, kr_ref, g_ref, wk_ref, wv_ref, gk_ref, cos_ref, sin_ref, lat_ref, k_ref, v_ref,
                *, mh, qk_dim):
    x = c_ref[...]
    c = x * lax.rsqrt(jnp.mean(x * x, axis=-1, keepdims=True) + EPS) * g_ref[...]
    lat_ref[...] = c
    cb = c.astype(BF16)
    kn = _dot(cb, wk_ref[...])
    v_ref[...] = _dot(cb, wv_ref[...]).astype(v_ref.dtype)
    kr = kr_ref[...]
    ssr = jnp.sum(kr * kr, axis=-1, keepdims=True)
    gk = gk_ref[...]
    cos = cos_ref[...]
    sin = sin_ref[...]
    lane = lax.broadcasted_iota(jnp.int32, cos.shape, 1)
    y = kr * gk[:, 128:]
    krot = y * cos + _swap_halves64(y, lane) * sin
    for h in range(mh):
        knh = kn[:, h * 128:(h + 1) * 128]
        rinv = lax.rsqrt((jnp.sum(knh * knh, axis=-1, keepdims=True) + ssr) * (1.0 / qk_dim) + EPS)
        k_ref[:, h * 256:h * 256 + 128] = (knh * rinv * gk[:, :128]).astype(k_ref.dtype)
        k_ref[:, h * 256 + 128:(h + 1) * 256] = (krot * rinv).astype(k_ref.dtype)


def _causal_attn_body(q_ref, k_ref, v_ref, o_ref, *, tq):
    seq = q_ref.shape[0]
    for qi in range(seq // tq):
        kv = (qi + 1) * tq
        rows = slice(qi * tq, kv)
        s = _nt(q_ref[rows, :], k_ref[0:kv, :])
        qpos = qi * tq + lax.broadcasted_iota(jnp.int32, s.shape, 0)
        kpos = lax.broadcasted_iota(jnp.int32, s.shape, 1)
        s = jnp.where(kpos <= qpos, s, -jnp.inf)
        p = jnp.exp(s - jnp.max(s, axis=-1, keepdims=True))
        l = jnp.sum(p, axis=-1, keepdims=True)
        o_ref[rows, :] = (_dot(p.astype(BF16), v_ref[0:kv, :]) / l).astype(o_ref.dtype)


def _sample_prep_body(q_ref, k_ref, wukt_ref, gk_ref, qabs_ref, qab_ref, ss_ref, *, mh, kvr):
    gk = gk_ref[...]
    db = q_ref.shape[0]
    lane = lax.broadcasted_iota(jnp.int32, (db, LANES), 1)
    sgn = jnp.where(lane < 32, 1.0, -1.0)
    cols = []
    for h in range(mh):
        qh = q_ref[:, h * 256:(h + 1) * 256].astype(F32)
        kh = k_ref[:, h * 256:(h + 1) * 256].astype(F32)
        cols.append(jnp.sum(qh * kh, axis=-1, keepdims=True))
        qn = qh[:, :128] * gk[:, :128]
        qabs = _dot(qn, wukt_ref[h], precision=lax.Precision.HIGHEST)
        qabs_ref[:, h * kvr:(h + 1) * kvr] = qabs.astype(qabs_ref.dtype)
        qraw = qh[:, 128:]
        qa = qraw * gk[:, 128:]
        qb = _swap_halves64(qraw, lane) * gk[:, 128:] * sgn
        qab_ref[:, h * 128:(h + 1) * 128] = (qa + pltpu.roll(qb, 64, 1)).astype(qab_ref.dtype)
    ss_ref[...] = jnp.concatenate(cols, axis=-1)


def _sample_attn_body(pt_ref, qabs_ref, qab_ref, ss_ref, cnew_ref, wukt_ref, ctab_ref, stab_ref, *rest,
                      npp, grp, mchunk, mh, nope, qk_dim, page):
    lat_refs = rest[:npp]
    krt_refs = rest[npp:2 * npp]
    o_ref = rest[2 * npp]
    m_scr, l_scr, acc_scr = rest[2 * npp + 1:]
    g = pl.program_id(1)

    @pl.when(g == 0)
    def _():
        m_scr[...] = ss_ref[0]
        l_scr[...] = jnp.ones(l_scr.shape, F32)
        acc_scr[...] = jnp.broadcast_to(cnew_ref[0], acc_scr.shape)

    qabs = qabs_ref[0]
    qab = qab_ref[0]
    n = grp * page
    hpc = mchunk // nope
    s_parts = []
    cgs = []
    for sub in range(npp // grp):
        ids = range(sub * grp, (sub + 1) * grp)
        cg = jnp.concatenate([lat_refs[r][...].astype(BF16) for r in ids], axis=0)
        krt = jnp.concatenate([krt_refs[r][...] for r in ids], axis=1)
        cols = pl.ds(sub * n, n)
        ssr = jnp.sum(krt * krt, axis=0, keepdims=True)
        rhs_rope = jnp.concatenate([krt * ctab_ref[:, cols], krt * stab_ref[:, cols]], axis=0).astype(BF16)
        sr = _dot(qab, rhs_rope)
        ss_parts = []
        for mc in range(mh * nope // mchunk):
            kt = _nt(wukt_ref[mc * mchunk:(mc + 1) * mchunk, :], cg)
            ss_parts.append(jnp.sum((kt * kt).reshape(hpc, nope, n), axis=1))
        ss = jnp.concatenate(ss_parts, axis=0) if len(ss_parts) > 1 else ss_parts[0]
        rinv = lax.rsqrt((ss + ssr) * (1.0 / qk_dim) + EPS)
        s_parts.append((_nt(qabs, cg) + sr) * rinv)
        cgs.append(cg)
    s = jnp.concatenate(s_parts, axis=1) if len(s_parts) > 1 else s_parts[0]
    m_old = m_scr[...]
    m_new = jnp.maximum(m_old, jnp.max(s, axis=-1, keepdims=True))
    corr = jnp.exp(m_old - m_new)
    p = jnp.exp(s - m_new)
    l_scr[...] = l_scr[...] * corr + jnp.sum(p, axis=-1, keepdims=True)
    acc = acc_scr[...] * corr
    for sub, cg in enumerate(cgs):
        acc = acc + _dot(p[:, sub * n:(sub + 1) * n].astype(BF16), cg)
    acc_scr[...] = acc
    m_scr[...] = m_new

    @pl.when(g == pl.num_programs(1) - 1)
    def _():
        o_ref[0] = acc_scr[...] / l_scr[...]


def _head_mm_body(x_ref, w_ref, o_ref):
    o_ref[...] = _dot(x_ref[...].astype(BF16), w_ref[0]).astype(o_ref.dtype)


def _merge_body(og_ref, om_ref, wg_ref, wm_ref, ga_ref, gb_ref, o_ref):
    ya = _dot(og_ref[...], wg_ref[...])
    yb = _dot(om_ref[...], wm_ref[...])
    o_ref[...] = (jax.nn.sigmoid(ga_ref[...]) * ya + jax.nn.sigmoid(gb_ref[...]) * yb).astype(o_ref.dtype)


def _peer_select_body(q_ref, sk_ref, s1_ref, s2_ref, st_ref, *, topk):
    q = q_ref[...].astype(BF16)
    sk = sk_ref[0].astype(BF16)
    s1 = _nt(sk[0], q[:, :128])
    s2 = _nt(sk[1], q[:, 128:])

    def top_values(s):
        vals = []
        for _ in range(topk):
            m = jnp.max(s, axis=0, keepdims=True)
            vals.append(m)
            s = jnp.where(s >= m, -jnp.inf, s)
        return vals

    a = top_values(s1)
    b = jnp.concatenate(top_values(s2), axis=0)
    cand = jnp.concatenate([a[r] + b for r in range(topk)], axis=0)
    best = top_values(cand)
    z = jnp.exp(best[0] - best[0])
    for r in range(1, topk):
        z = z + jnp.exp(best[r] - best[0])
    s1_ref[0] = s1
    s2_ref[0] = s2
    ts = s1.shape[1]
    st_ref[0] = jnp.concatenate([best[topk - 1], best[0] + jnp.log(z), jnp.zeros((6, ts), F32)], axis=0)


def _peer_dense_body(x_ref, u_ref, v_ref, s1_ref, s2_ref, st_ref, o_ref, p_scr, wt_scr, *, ph, te, n_eblocks):
    j = pl.program_id(1)
    rd = (j + 1) % 2
    wr = j % 2

    @pl.when(j == 0)
    def _():
        o_ref[...] = jnp.zeros(o_ref.shape, F32)
        p_scr[1] = jnp.zeros(p_scr.shape[1:], p_scr.dtype)

    tm, d = x_ref.shape
    na = te // LANES
    nlc = tm // LANES
    kc = d // nlc
    jc = jnp.minimum(j, n_eblocks - 1)
    s1_rows = [[s1_ref[h, pl.ds(jc * na + a, 1), :] for h in range(ph)] for a in range(na)]

    ht = None
    for lc in range(nlc):
        ks = slice(lc * kc, (lc + 1) * kc)
        part = _nt(u_ref[:, ks], x_ref[:, ks])
        ht = part if ht is None else ht + part
        ls = slice(lc * LANES, (lc + 1) * LANES)
        for a in range(na):
            wt = None
            for h in range(ph):
                ssum = s2_ref[h, :, ls] + s1_rows[a][h][:, ls]
                term = jnp.where(ssum >= st_ref[h, 0:1, ls], jnp.exp(ssum - st_ref[h, 1:2, ls]), 0.0)
                wt = term if wt is None else wt + term
            wt_scr[a * LANES:(a + 1) * LANES, ls] = wt

    p_old = p_scr[rd]
    for lc in range(nlc):
        cs = slice(lc * kc, (lc + 1) * kc)
        o_ref[:, cs] += _dot(p_old, v_ref[:, cs])
        ls = slice(lc * LANES, (lc + 1) * LANES)
        hta = ht[:, ls]
        blk = wt_scr[:, ls] * (0.5 * hta * (1.0 + lax.erf(hta * (2.0 ** -0.5))))
        p_scr[wr, ls, :] = blk.T.astype(p_scr.dtype)


def _peer_dense(hf, u, v, s1t, s2t, stat, row0, nrows, tm, te):
    d = hf.shape[1]
    n_exp = u.shape[0]
    ph, nkeys, _ = s1t.shape
    nb = n_exp // te
    blk0 = row0 // tm
    assert row0 % tm == 0 and nrows % tm == 0
    return pl.pallas_call(
        functools.partial(_peer_dense_body, ph=ph, te=te, n_eblocks=nb),
        grid=(nrows // tm, nb + 1),
        in_specs=[pl.BlockSpec((tm, d), lambda i, j: (i + blk0, 0)),
                  pl.BlockSpec((te, d), lambda i, j: (jnp.minimum(j, nb - 1), 0)),
                  pl.BlockSpec((te, d), lambda i, j: (jnp.maximum(j - 1, 0), 0)),
                  pl.BlockSpec((ph, nkeys, tm), lambda i, j: (0, 0, i + blk0)),
                  pl.BlockSpec((ph, nkeys, tm), lambda i, j: (0, 0, i + blk0)),
                  pl.BlockSpec((ph, 8, tm), lambda i, j: (0, 0, i + blk0))],
        out_specs=pl.BlockSpec((tm, d), lambda i, j: (i, 0)),
        out_shape=jax.ShapeDtypeStruct((nrows, d), F32),
        scratch_shapes=[pltpu.VMEM((2, tm, te), BF16), pltpu.VMEM((te, tm), F32)],
        compiler_params=_cp(("parallel", "arbitrary")),
        name="peer_dense",
    )(hf, u, v, s1t, s2t, stat)


def _add_body(a_ref, b_ref, o_ref):
    o_ref[...] = a_ref[...] + b_ref[...]


def _add_rows(a, b, row0, nrows, tm):
    d = a.shape[1]
    blk0 = row0 // tm
    return pl.pallas_call(
        _add_body,
        grid=(nrows // tm,),
        in_specs=[pl.BlockSpec((tm, d), lambda i: (i + blk0, 0)),
                  pl.BlockSpec((tm, d), lambda i: (i, 0))],
        out_specs=pl.BlockSpec((tm, d), lambda i: (i, 0)),
        out_shape=jax.ShapeDtypeStruct((nrows, d), F32),
        compiler_params=_cp(("parallel",)),
        name="residual_add",
    )(a, b)


def _rope_tables(pos, rope):
    half = rope // 2
    inv = jnp.power(ROPE_THETA, -2.0 * jnp.arange(half, dtype=F32) / rope)
    ang = pos.astype(F32)[:, None] * inv[None, :]
    return jnp.cos(ang), jnp.sin(ang)


def kernel(x_prompt, x_sample, cache_latent, cache_krope, page_table, state_gdn, state_conv, norm_mix_g, w_in,
           gdn_conv_w, gdn_a_log, gdn_dt_bias, gdn_norm_g, gdn_out, mla_q_norm_g, mla_kv_norm_g, mla_w_uq,
           mla_w_uk, mla_w_uv, mla_qk_norm_q, mla_qk_norm_k, mla_out, w_o, norm_ffn_g, peer_w_q, peer_subkeys,
           peer_u, peer_v):
    depth = w_in.shape[0]
    assert depth == 1, "single-layer trunk"
    bsz, seq, d = x_prompt.shape
    db, dseq, _ = x_sample.shape
    assert dseq == 1
    tp = bsz * seq
    t = tp + db
    page = cache_latent.shape[2]
    kvr = cache_latent.shape[3]
    rope = cache_krope.shape[3]
    n_pages = page_table.shape[1]
    past_len = n_pages * page
    gh, dk, dv = state_gdn.shape[2:]
    conv = gdn_conv_w.shape[1]
    cw = gdn_conv_w.shape[2]
    qkw = gh * dk
    vw = gh * dv
    qr = mla_q_norm_g.shape[1]
    mh, nope = mla_w_uk.shape[2:]
    vdim = mla_w_uv.shape[3]
    qk_dim = nope + rope
    ph, _, nkeys, qhalf = peer_subkeys.shape[1:]
    n_exp = peer_u.shape[1]
    assert dk == LANES and dv == LANES and nope == LANES and vdim == LANES and rope == 64
    assert nkeys == LANES and qhalf == LANES and page == LANES and n_exp == nkeys * nkeys
    assert tp % db == 0 and seq % GDN_CHUNK == 0

    hb = _pick(gh, (4, 2, 1))
    ngrp = gh // hb
    assert hb <= 64

    sizes = (cw, gh, gh, vw, qr, kvr, rope, d, d)
    offs = [0]
    for s in sizes:
        offs.append(offs[-1] + s)
    w_in0 = w_in[0]

    def wcol(i):
        return w_in0[:, offs[i]:offs[i + 1]]

    w_big = jnp.concatenate([wcol(0), wcol(3), wcol(7), wcol(8)], axis=1).astype(BF16)
    off_zg, off_ga, off_gb = cw, cw + vw, cw + vw + d
    wa = wcol(1).reshape(d, ngrp, hb)
    wb = wcol(2).reshape(d, ngrp, hb)
    w_ab = jnp.concatenate([wa, jnp.zeros((d, ngrp, 64 - hb), F32), wb, jnp.zeros((d, ngrp, 64 - hb), F32)],
                           axis=2).reshape(d, ngrp * LANES)
    w_small = jnp.concatenate([wcol(4), wcol(5), wcol(6), jnp.zeros((d, LANES - rope), F32), w_ab],
                              axis=1).astype(BF16)

    def group_lanes(vec):
        return jnp.pad(vec.reshape(ngrp, hb), ((0, 0), (0, LANES - hb))).reshape(ngrp * LANES)

    par = jnp.stack([group_lanes(gdn_a_log[0]), group_lanes(gdn_dt_bias[0])], axis=0)

    xp2 = x_prompt.reshape(tp, d)
    xs2 = x_sample.reshape(db, d)
    (xn,) = _rmsnorm_rows2(xp2, xs2, norm_mix_g[0])
    z_big = _matmul(xn, w_big, F32, name="in_proj_big")
    c_q, c_kv_raw, kr_pad, ab = _proj_multi(xn, w_small, (qr, kvr, LANES, ngrp * LANES))

    cwb = _pick(qkw, (512, 256, 128))
    nq_blocks = qkw // cwb
    conv_w = gdn_conv_w[0]
    qkv_act_p, cv_p = pl.pallas_call(
        functools.partial(_conv_prompt_body, nq_blocks=nq_blocks, dk=dk, conv=conv),
        grid=(bsz, cw // cwb),
        in_specs=[pl.BlockSpec((seq, cwb), lambda b, c: (b, c)),
                  pl.BlockSpec((conv, cwb), lambda b, c: (0, c))],
        out_specs=[pl.BlockSpec((seq, cwb), lambda b, c: (b, c)),
                   pl.BlockSpec((1, conv - 1, cwb), lambda b, c: (b, 0, c))],
        out_shape=[jax.ShapeDtypeStruct((tp, cw), BF16),
                   jax.ShapeDtypeStruct((bsz, conv - 1, cw), F32)],
        compiler_params=_cp(("parallel", "parallel")),
        name="gdn_conv_prompt",
    )(z_big, conv_w)

    st_conv2 = state_conv[0].reshape(db, (conv - 1) * cw)
    ncb = cw // cwb
    srow = tp // db
    qkv_act_s = pl.pallas_call(
        functools.partial(_conv_sample_body, nq_blocks=nq_blocks, dk=dk, conv=conv),
        grid=(ncb,),
        in_specs=[pl.BlockSpec((db, cwb), functools.partial(lambda c, j: (0, j * ncb + c), j=j))
                  for j in range(conv - 1)]
                 + [pl.BlockSpec((db, cwb), lambda c: (srow, c)),
                    pl.BlockSpec((conv, cwb), lambda c: (0, c))],
        out_specs=pl.BlockSpec((db, cwb), lambda c: (0, c)),
        out_shape=jax.ShapeDtypeStruct((db, cw), F32),
        compiler_params=_cp(("parallel",)),
        name="gdn_conv_sample",
    )(*([st_conv2] * (conv - 1)), z_big, conv_w)

    hw = hb * dk
    assert off_zg % hw == 0 and qkw % hw == 0
    par3 = par.reshape(2, ngrp, LANES).transpose(1, 0, 2)
    gnorm = gdn_norm_g[0].reshape(1, dv)
    o_gdn_p, s_gdn_p = pl.pallas_call(
        functools.partial(_gdn_prompt_body, hb=hb, chunk=GDN_CHUNK, dk=dk),
        grid=(bsz, ngrp),
        in_specs=[pl.BlockSpec((seq, hw), lambda b, g: (b, g)),
                  pl.BlockSpec((seq, hw), lambda b, g: (b, ngrp + g)),
                  pl.BlockSpec((seq, hw), lambda b, g: (b, 2 * ngrp + g)),
                  pl.BlockSpec((seq, hw), lambda b, g: (b, off_zg // hw + g)),
                  pl.BlockSpec((seq, LANES), lambda b, g: (b, g)),
                  pl.BlockSpec((1, 2, LANES), lambda b, g: (g, 0, 0)),
                  pl.BlockSpec((1, dv), lambda b, g: (0, 0))],
        out_specs=[pl.BlockSpec((seq, hw), lambda b, g: (b, g)),
                   pl.BlockSpec((1, hb, dk, dv), lambda b, g: (b, g, 0, 0))],
        out_shape=[jax.ShapeDtypeStruct((tp, vw), BF16),
                   jax.ShapeDtypeStruct((bsz, gh, dk, dv), F32)],
        scratch_shapes=[pltpu.VMEM((seq, hw), F32), pltpu.VMEM((seq, hw), BF16), pltpu.VMEM((seq, hw), BF16),
                        pltpu.VMEM((seq, hw), BF16), pltpu.VMEM((hb, seq, GDN_CHUNK), BF16),
                        pltpu.VMEM((seq // GDN_CHUNK, LANES), F32)],
        compiler_params=_cp(("parallel", "parallel")),
        name="gdn_prompt",
    )(qkv_act_p, qkv_act_p, qkv_act_p, z_big, ab, par3, gnorm)

    qkv_s3 = qkv_act_s.reshape(db, 3 * gh, dk)
    qkt_s = jnp.swapaxes(qkv_s3[:, :2 * gh, :], 1, 2)
    ab_s = ab[tp:].reshape(db, 1, ngrp * LANES)
    zg_s = z_big[tp:, off_zg:off_zg + vw].reshape(db, 1, vw)
    o_gdn_s, s_gdn_s = pl.pallas_call(
        functools.partial(_gdn_sample_body, gh=gh, hb=hb, dk=dk),
        grid=(db,),
        in_specs=[pl.BlockSpec((1, 3 * gh, dk), lambda b: (b, 0, 0)),
                  pl.BlockSpec((1, dk, 2 * gh), lambda b: (b, 0, 0)),
                  pl.BlockSpec((1, 1, ngrp * LANES), lambda b: (b, 0, 0)),
                  pl.BlockSpec((2, ngrp * LANES), lambda b: (0, 0)),
                  pl.BlockSpec((1, 1, vw), lambda b: (b, 0, 0)),
                  pl.BlockSpec((1, dv), lambda b: (0, 0)),
                  pl.BlockSpec((1, gh, dk, dv), lambda b: (b, 0, 0, 0))],
        out_specs=[pl.BlockSpec((1, 1, vw), lambda b: (b, 0, 0)),
                   pl.BlockSpec((1, gh, dk, dv), lambda b: (b, 0, 0, 0))],
        out_shape=[jax.ShapeDtypeStruct((db, 1, vw), BF16),
                   jax.ShapeDtypeStruct((db, gh, dk, dv), F32)],
        compiler_params=_cp(("parallel",)),
        name="gdn_sample",
    )(qkv_s3, qkt_s, ab_s, par, zg_s, gnorm, state_gdn[0].astype(F32))
    o_gdn = jnp.concatenate([o_gdn_p, o_gdn_s.reshape(db, vw)], axis=0)

    pos_all = jnp.concatenate([jnp.tile(jnp.arange(seq, dtype=jnp.int32), bsz),
                               jnp.full((db,), past_len, jnp.int32)])
    cos_h, sin_h = _rope_tables(pos_all, rope)
    zpad = jnp.zeros((t, LANES - rope), F32)
    cos_t = jnp.concatenate([cos_h, cos_h, zpad], axis=1)
    sin_t = jnp.concatenate([-sin_h, sin_h, zpad], axis=1)

    w_uq3 = mla_w_uq[0].reshape(qr, mh, qk_dim)
    w_uq_pad = jnp.concatenate([w_uq3, jnp.zeros((qr, mh, 256 - qk_dim), F32)], axis=2).reshape(qr, mh * 256)
    w_uq_pad = w_uq_pad.astype(BF16)

    def gain256(gvec):
        return jnp.concatenate([gvec, jnp.zeros((256 - qk_dim,), F32)]).reshape(1, 256)

    tmq = _pick(t, (320, 256, 128, 64, 8))
    sm_scale = qk_dim ** -0.5
    q_all = pl.pallas_call(
        functools.partial(_mla_q_body, mh=mh, qk_dim=qk_dim, scale=sm_scale),
        grid=(t // tmq,),
        in_specs=[pl.BlockSpec((tmq, qr), lambda i: (i, 0)),
                  pl.BlockSpec((1, qr), lambda i: (0, 0)),
                  pl.BlockSpec((qr, mh * 256), lambda i: (0, 0)),
                  pl.BlockSpec((1, 256), lambda i: (0, 0)),
                  pl.BlockSpec((tmq, LANES), lambda i: (i, 0)),
                  pl.BlockSpec((tmq, LANES), lambda i: (i, 0))],
        out_specs=pl.BlockSpec((tmq, mh * 256), lambda i: (i, 0)),
        out_shape=jax.ShapeDtypeStruct((t, mh * 256), BF16),
        compiler_params=_cp(("parallel",)),
        name="mla_q",
    )(c_q, mla_q_norm_g[0].reshape(1, qr), w_uq_pad, gain256(mla_qk_norm_q[0]), cos_t, sin_t)

    w_uk2 = mla_w_uk[0].reshape(kvr, mh * nope).astype(BF16)
    w_uv2 = mla_w_uv[0].reshape(kvr, mh * vdim).astype(BF16)
    gk256 = gain256(mla_qk_norm_k[0])
    lat_all, k_all, v_all = pl.pallas_call(
        functools.partial(_mla_k_body, mh=mh, qk_dim=qk_dim),
        grid=(t // tmq,),
        in_specs=[pl.BlockSpec((tmq, kvr), lambda i: (i, 0)),
                  pl.BlockSpec((tmq, LANES), lambda i: (i, 0)),
                  pl.BlockSpec((1, kvr), lambda i: (0, 0)),
                  pl.BlockSpec((kvr, mh * nope), lambda i: (0, 0)),
                  pl.BlockSpec((kvr, mh * vdim), lambda i: (0, 0)),
                  pl.BlockSpec((1, 256), lambda i: (0, 0)),
                  pl.BlockSpec((tmq, LANES), lambda i: (i, 0)),
                  pl.BlockSpec((tmq, LANES), lambda i: (i, 0))],
        out_specs=[pl.BlockSpec((tmq, kvr), lambda i: (i, 0)),
                   pl.BlockSpec((tmq, mh * 256), lambda i: (i, 0)),
                   pl.BlockSpec((tmq, mh * vdim), lambda i: (i, 0))],
        out_shape=[jax.ShapeDtypeStruct((t, kvr), F32),
                   jax.ShapeDtypeStruct((t, mh * 256), BF16),
                   jax.ShapeDtypeStruct((t, mh * vdim), BF16)],
        compiler_params=_cp(("parallel",)),
        name="mla_k",
    )(c_kv_raw, kr_pad, mla_kv_norm_g[0].reshape(1, kvr), w_uk2, w_uv2, gk256, cos_t, sin_t)

    tq = _pick(seq, (512, 256, 128))
    o_mla_p = pl.pallas_call(
        functools.partial(_causal_attn_body, tq=tq),
        grid=(bsz, mh),
        in_specs=[pl.BlockSpec((seq, 256), lambda b, h: (b, h)),
                  pl.BlockSpec((seq, 256), lambda b, h: (b, h)),
                  pl.BlockSpec((seq, vdim), lambda b, h: (b, h))],
        out_specs=pl.BlockSpec((seq, vdim), lambda b, h: (b, h)),
        out_shape=jax.ShapeDtypeStruct((tp, mh * vdim), BF16),
        compiler_params=_cp(("parallel", "parallel")),
        name="mla_attn_prompt",
    )(q_all, k_all, v_all)

    w_ukt = jnp.transpose(mla_w_uk[0], (1, 2, 0))
    q_s = q_all[tp:]
    k_s = k_all[tp:]
    qabs, qab, s_self = pl.pallas_call(
        functools.partial(_sample_prep_body, mh=mh, kvr=kvr),
        out_shape=[jax.ShapeDtypeStruct((db, mh * kvr), BF16),
                   jax.ShapeDtypeStruct((db, mh * LANES), BF16),
                   jax.ShapeDtypeStruct((db, mh), F32)],
        compiler_params=pltpu.CompilerParams(vmem_limit_bytes=VMEM_LIMIT),
        name="mla_sample_prep",
    )(q_s, k_s, w_ukt, gk256)

    npp = _pick(n_pages, (16, 8, 4, 2, 1))
    grp = _pick(npp, (4, 2, 1))
    mchunk = _pick(mh * nope, (512, 256, 128))
    pos_past = jnp.arange(past_len, dtype=jnp.int32)
    cos_p, sin_p = _rope_tables(pos_past, rope)
    ctab = jnp.concatenate([cos_p, cos_p], axis=1).T
    stab = jnp.concatenate([sin_p, sin_p], axis=1).T
    w_ukt2 = w_ukt.reshape(mh * nope, kvr).astype(BF16)
    lat0 = cache_latent[0]
    krt0 = jnp.swapaxes(cache_krope[0], 1, 2)

    def page_map(r):
        return lambda b, g, pt: (pt[b * n_pages + g * npp + r], 0, 0)

    lat_s = pl.pallas_call(
        functools.partial(_sample_attn_body, npp=npp, grp=grp, mchunk=mchunk, mh=mh, nope=nope, qk_dim=qk_dim,
                          page=page),
        grid_spec=pltpu.PrefetchScalarGridSpec(
            num_scalar_prefetch=1,
            grid=(db, n_pages // npp),
            in_specs=[pl.BlockSpec((1, mh, kvr), lambda b, g, pt: (b, 0, 0)),
                      pl.BlockSpec((1, mh, LANES), lambda b, g, pt: (b, 0, 0)),
                      pl.BlockSpec((1, mh, 1), lambda b, g, pt: (b, 0, 0)),
                      pl.BlockSpec((1, 1, kvr), lambda b, g, pt: (b, 0, 0)),
                      pl.BlockSpec((mh * nope, kvr), lambda b, g, pt: (0, 0)),
                      pl.BlockSpec((rope, npp * page), lambda b, g, pt: (0, g)),
                      pl.BlockSpec((rope, npp * page), lambda b, g, pt: (0, g))]
                     + [pl.BlockSpec((None, page, kvr), page_map(r)) for r in range(npp)]
                     + [pl.BlockSpec((None, rope, page), page_map(r)) for r in range(npp)],
            out_specs=pl.BlockSpec((1, mh, kvr), lambda b, g, pt: (b, 0, 0)),
            scratch_shapes=[pltpu.VMEM((mh, 1), F32), pltpu.VMEM((mh, 1), F32), pltpu.VMEM((mh, kvr), F32)],
        ),
        out_shape=jax.ShapeDtypeStruct((db, mh, kvr), F32),
        compiler_params=_cp(("parallel", "arbitrary")),
        name="mla_sample_attn",
    )(page_table.reshape(-1).astype(jnp.int32),
      qabs.reshape(db, mh, kvr), qab.reshape(db, mh, LANES), s_self.reshape(db, mh, 1),
      lat_all[tp:].reshape(db, 1, kvr), w_ukt2, ctab, stab,
      *([lat0] * npp), *([krt0] * npp))

    w_uv3 = jnp.transpose(mla_w_uv[0], (1, 0, 2)).astype(BF16)
    o_mla_s = pl.pallas_call(
        _head_mm_body,
        grid=(mh,),
        in_specs=[pl.BlockSpec((db, kvr), lambda h: (0, h)),
                  pl.BlockSpec((1, kvr, vdim), lambda h: (h, 0, 0))],
        out_specs=pl.BlockSpec((db, vdim), lambda h: (0, h)),
        out_shape=jax.ShapeDtypeStruct((db, mh * vdim), BF16),
        compiler_params=_cp(("parallel",)),
        name="mla_sample_uv",
    )(lat_s.reshape(db, mh * kvr), w_uv3)
    o_mla = jnp.concatenate([o_mla_p, o_mla_s], axis=0)

    tm = _pick(t, (640, 512, 256, 128, 64, 8))
    tn = _pick(d, (512, 256, 128))
    assert off_ga % tn == 0 and off_gb % tn == 0
    merged = pl.pallas_call(
        _merge_body,
        grid=(d // tn, t // tm),
        in_specs=[pl.BlockSpec((tm, vw), lambda j, i: (i, 0)),
                  pl.BlockSpec((tm, mh * vdim), lambda j, i: (i, 0)),
                  pl.BlockSpec((vw, tn), lambda j, i: (0, j)),
                  pl.BlockSpec((mh * vdim, tn), lambda j, i: (0, j)),
                  pl.BlockSpec((tm, tn), lambda j, i: (i, off_ga // tn + j)),
                  pl.BlockSpec((tm, tn), lambda j, i: (i, off_gb // tn + j))],
        out_specs=pl.BlockSpec((tm, tn), lambda j, i: (i, j)),
        out_shape=jax.ShapeDtypeStruct((t, d), BF16),
        compiler_params=_cp(("parallel", "parallel")),
        name="mixer_merge",
    )(o_gdn, o_mla, gdn_out[0].astype(BF16), mla_out[0].astype(BF16), z_big, z_big)
    y_o = _matmul(merged, w_o[0].astype(BF16), F32, name="out_proj")

    x2, hf = _rmsnorm_rows2(xp2, xs2, norm_ffn_g[0], branch=y_o)
    pq = _matmul(hf, peer_w_q[0].astype(BF16), F32, name="peer_query")
    ts = _pick(t, (128,))
    s1t, s2t, stat = pl.pallas_call(
        functools.partial(_peer_select_body, topk=PEER_TOPK),
        grid=(t // ts, ph),
        in_specs=[pl.BlockSpec((ts, 2 * qhalf), lambda i, h: (i, h)),
                  pl.BlockSpec((1, 2, nkeys, qhalf), lambda i, h: (h, 0, 0, 0))],
        out_specs=[pl.BlockSpec((1, nkeys, ts), lambda i, h: (h, 0, i)),
                   pl.BlockSpec((1, nkeys, ts), lambda i, h: (h, 0, i)),
                   pl.BlockSpec((1, 8, ts), lambda i, h: (h, 0, i))],
        out_shape=[jax.ShapeDtypeStruct((ph, nkeys, t), F32),
                   jax.ShapeDtypeStruct((ph, nkeys, t), F32),
                   jax.ShapeDtypeStruct((ph, 8, t), F32)],
        compiler_params=_cp(("parallel", "parallel")),
        name="peer_select",
    )(pq, peer_subkeys[0])

    peer_u16 = peer_u[0].astype(BF16)
    peer_v16 = peer_v[0].astype(BF16)
    te = _pick(n_exp, (512, 256, 128))
    tmp = _pick(tp, (512, 256, 128))
    tms = _pick(db, (128,))
    assert tp % tms == 0
    peer_p = _peer_dense(hf, peer_u16, peer_v16, s1t, s2t, stat, 0, tp, tmp, te)
    peer_s = _peer_dense(hf, peer_u16, peer_v16, s1t, s2t, stat, tp, db, tms, te)

    tma = _pick(db, (128, 64, 8))
    assert tp % tma == 0
    y_p = _add_rows(x2, peer_p, 0, tp, tma).reshape(bsz, seq, d)
    y_s = _add_rows(x2, peer_s, tp, db, tma).reshape(db, 1, d)

    lat_p = lat_all[:tp].reshape(1, bsz, seq, kvr)
    lat_sn = lat_all[tp:].reshape(1, db, 1, kvr)
    kr_p = kr_pad[:tp, :rope].reshape(1, bsz, seq, rope)
    kr_s = kr_pad[tp:, :rope].reshape(1, db, 1, rope)
    cv_p = cv_p[None]
    cv_s = jnp.concatenate([state_conv[0][:, 1:, :], z_big[tp:, :cw].reshape(db, 1, cw)], axis=1)[None]
    return (y_p, y_s, lat_p, kr_p, lat_sn, kr_s,
            s_gdn_p[None].astype(state_gdn.dtype), s_gdn_s[None].astype(state_gdn.dtype), cv_p, cv_s)
```

```python
import functools
import math

import jax
import jax.numpy as jnp
from jax import lax
from jax.experimental import pallas as pl
from jax.experimental.pallas import tpu as pltpu

F32 = jnp.float32
BF16 = jnp.bfloat16

EPS = 1e-6
ROPE_THETA = 10000.0
GDN_CHUNK = 64
PEER_TOPK = 16
LANES = 128
VMEM_LIMIT = 56 * 1024 * 1024


def _cp(dims, vmem=VMEM_LIMIT):
    return pltpu.CompilerParams(dimension_semantics=dims, vmem_limit_bytes=vmem)


def _pick(n, cands):
    for c in cands:
        if c <= n and n % c == 0:
            return c
    return n


def _nt(a, b, precision=None):
    return lax.dot_general(a, b, (((1,), (1,)), ((), ())),
                           preferred_element_type=F32, precision=precision)


def _tn(a, b):
    return lax.dot_general(a, b, (((0,), (0,)), ((), ())), preferred_element_type=F32)


def _dot(a, b, precision=None):
    return jnp.dot(a, b, preferred_element_type=F32, precision=precision)


def _softplus(x):
    return jnp.maximum(x, 0.0) + jnp.log1p(jnp.exp(-jnp.abs(x)))


def _swap_halves64(y, lane):
    return jnp.where(lane < 32, pltpu.roll(y, 96, 1), pltpu.roll(y, 32, 1))


def _rmsnorm2_body(xp_ref, xs_ref, g_ref, *rest, np_blocks, with_residual):
    if with_residual:
        y_ref, x2_ref, o_ref = rest
    else:
        (o_ref,) = rest
    i = pl.program_id(0)

    def emit(x):
        if with_residual:
            x = x + y_ref[...]
            x2_ref[...] = x
        y = x * lax.rsqrt(jnp.mean(x * x, axis=-1, keepdims=True) + EPS)
        o_ref[...] = (y * g_ref[...]).astype(o_ref.dtype)

    @pl.when(i < np_blocks)
    def _():
        emit(xp_ref[...])

    @pl.when(i >= np_blocks)
    def _():
        emit(xs_ref[...])


def _rmsnorm_rows2(xp, xs, g, branch=None):
    tp, d = xp.shape
    db = xs.shape[0]
    tm = _pick(db, (128, 64, 8))
    assert tp % tm == 0 and db % tm == 0
    npb = tp // tm
    t = tp + db
    row = pl.BlockSpec((tm, d), lambda i: (i, 0))
    in_specs = [pl.BlockSpec((tm, d), lambda i: (jnp.minimum(i, npb - 1), 0)),
                pl.BlockSpec((tm, d), lambda i: (jnp.maximum(i - npb, 0), 0)),
                pl.BlockSpec((1, d), lambda i: (0, 0))]
    args = [xp, xs, g.reshape(1, d)]
    out_specs = [row]
    out_shape = [jax.ShapeDtypeStruct((t, d), BF16)]
    if branch is not None:
        in_specs.append(row)
        args.append(branch)
        out_specs = [row, row]
        out_shape = [jax.ShapeDtypeStruct((t, d), F32)] + out_shape
    return pl.pallas_call(
        functools.partial(_rmsnorm2_body, np_blocks=npb, with_residual=branch is not None),
        grid=(t // tm,),
        in_specs=in_specs,
        out_specs=out_specs,
        out_shape=out_shape,
        compiler_params=_cp(("parallel",)),
        name="rmsnorm_rows",
    )(*args)


def _mm_body(x_ref, w_ref, o_ref):
    o_ref[...] = _dot(x_ref[...], w_ref[...]).astype(o_ref.dtype)


def _matmul(x, w, out_dtype, name="matmul"):
    m, k = x.shape
    n = w.shape[1]
    tm = _pick(m, (640, 512, 256, 128, 64, 8))
    tn = _pick(n, (512, 256, 128))
    return pl.pallas_call(
        _mm_body,
        grid=(n // tn, m // tm),
        in_specs=[pl.BlockSpec((tm, k), lambda j, i: (i, 0)),
                  pl.BlockSpec((k, tn), lambda j, i: (0, j))],
        out_specs=pl.BlockSpec((tm, tn), lambda j, i: (i, j)),
        out_shape=jax.ShapeDtypeStruct((m, n), out_dtype),
        compiler_params=_cp(("parallel", "parallel")),
        name=name,
    )(x, w)


def _proj_multi_body(x_ref, w_ref, *o_refs, widths):
    y = _dot(x_ref[...], w_ref[...])
    off = 0
    for o_ref, wd in zip(o_refs, widths):
        o_ref[...] = y[:, off:off + wd]
        off += wd


def _proj_multi(x, w, widths):
    m, k = x.shape
    n = w.shape[1]
    tm = _pick(m, (640, 512, 256, 128, 64, 8))
    return pl.pallas_call(
        functools.partial(_proj_multi_body, widths=widths),
        grid=(m // tm,),
        in_specs=[pl.BlockSpec((tm, k), lambda i: (i, 0)),
                  pl.BlockSpec((k, n), lambda i: (0, 0))],
        out_specs=[pl.BlockSpec((tm, wd), lambda i: (i, 0)) for wd in widths],
        out_shape=[jax.ShapeDtypeStruct((m, wd), F32) for wd in widths],
        compiler_params=_cp(("parallel",)),
        name="proj_multi",
    )(x, w)


def _conv_finish(acc, o_ref, c, nq_blocks, dk):
    y = acc * jax.nn.sigmoid(acc)
    cw = y.shape[1]

    def normed(scale):
        parts = []
        for s in range(cw // dk):
            seg = y[:, s * dk:(s + 1) * dk]
            parts.append(seg * (lax.rsqrt(jnp.sum(seg * seg, axis=-1, keepdims=True) + EPS) * scale))
        return jnp.concatenate(parts, axis=-1) if len(parts) > 1 else parts[0]

    @pl.when(c < nq_blocks)
    def _():
        o_ref[...] = normed(dk ** -0.5).astype(o_ref.dtype)

    @pl.when(jnp.logical_and(c >= nq_blocks, c < 2 * nq_blocks))
    def _():
        o_ref[...] = normed(1.0).astype(o_ref.dtype)

    @pl.when(c >= 2 * nq_blocks)
    def _():
        o_ref[...] = y.astype(o_ref.dtype)


def _conv_prompt_body(x_ref, w_ref, o_ref, tail_ref, *, nq_blocks, dk, conv):
    x = x_ref[...]
    w = w_ref[...]
    tail_ref[0] = x[x.shape[0] - (conv - 1):, :]
    row = lax.broadcasted_iota(jnp.int32, x.shape, 0)
    acc = None
    for j in range(conv):
        sh = conv - 1 - j
        xs = x if sh == 0 else jnp.where(row >= sh, pltpu.roll(x, sh, 0), 0.0)
        term = xs * w[j:j + 1, :]
        acc = term if acc is None else acc + term
    _conv_finish(acc, o_ref, pl.program_id(1), nq_blocks, dk)


def _conv_sample_body(*refs, nq_blocks, dk, conv):
    hist = refs[:conv - 1]
    x_ref, w_ref, o_ref = refs[conv - 1:]
    w = w_ref[...]
    acc = None
    for j in range(conv):
        src = x_ref[...] if j == conv - 1 else hist[j][...]
        term = src * w[j:j + 1, :]
        acc = term if acc is None else acc + term
    _conv_finish(acc, o_ref, pl.program_id(0), nq_blocks, dk)


def _gdn_prompt_body(q_ref, k_ref, v_ref, zg_ref, ab_ref, par_ref, gn_ref, o_ref, s_ref,
                     u_scr, w_scr, qg_scr, kd_scr, qk_scr, dl_scr, *, hb, chunk, dk):
    seq = q_ref.shape[0]
    n_chunks = seq // chunk
    s_ref[...] = jnp.zeros(s_ref.shape, F32)
    ii = lax.broadcasted_iota(jnp.int32, (chunk, chunk), 0)
    jj = lax.broadcasted_iota(jnp.int32, (chunk, chunk), 1)
    causal = ii >= jj
    strict = ii > jj
    rowi = lax.broadcasted_iota(jnp.int32, (chunk, LANES), 0)
    par = par_ref[0]
    a_log = par[0:1, :]
    dt_bias = par[1:2, :]
    gnorm = gn_ref[...]
    n_double = int(math.log2(chunk)) - 1

    def prep_step(ci, carry):
        r0 = pl.multiple_of(ci * chunk, chunk)
        rows = pl.ds(r0, chunk)
        ab = ab_ref[rows, :]
        g = -jnp.exp(a_log) * _softplus(ab + dt_bias)
        beta = jax.nn.sigmoid(ab)
        gc = g
        sh = 1
        while sh < chunk:
            gc = gc + jnp.where(rowi >= sh, pltpu.roll(gc, sh, 0), 0.0)
            sh *= 2
        gct = gc.T
        dl_scr[pl.ds(ci, 1), :] = jnp.exp(gc[chunk - 1:chunk, :])
        heads = range(hb)
        xs, kbs, vbs, gcols = [], [], [], []
        for hh in heads:
            cs = slice(hh * dk, (hh + 1) * dk)
            gcol = gc[:, hh:hh + 1]
            bcol = beta[:, 64 + hh:65 + hh]
            q = q_ref[rows, cs].astype(F32)
            k = k_ref[rows, cs].astype(F32)
            kbf = k_ref[rows, cs].astype(BF16)
            decay = jnp.exp(jnp.where(causal, gcol - gct[hh:hh + 1, :], -jnp.inf))
            kb = k * bcol
            xs.append(-jnp.where(strict, _nt(kb.astype(BF16), kbf) * decay, 0.0))
            qk = jnp.where(causal, _nt(q_ref[rows, cs].astype(BF16), kbf) * decay, 0.0)
            qk_scr[hh, rows, :] = qk.astype(BF16)
            g_last = gc[chunk - 1:chunk, hh:hh + 1]
            qg_scr[rows, cs] = (q * jnp.exp(gcol)).astype(BF16)
            kd_scr[rows, cs] = (k * jnp.exp(g_last - gcol)).astype(BF16)
            kbs.append(kb)
            vbs.append(v_ref[rows, cs].astype(F32) * bcol)
            gcols.append(gcol)
        ns = list(xs)
        ps = list(xs)
        for _ in range(n_double):
            pb = [p.astype(BF16) for p in ps]
            ps = [_dot(b_, b_) for b_ in pb]
            ns = [n_ + p_ + _dot(n_.astype(BF16), p_.astype(BF16)) for n_, p_ in zip(ns, ps)]
        for hh in heads:
            cs = slice(hh * dk, (hh + 1) * dk)
            nb = ns[hh].astype(BF16)
            u_scr[rows, cs] = vbs[hh] + _dot(nb, vbs[hh].astype(BF16))
            kg = kbs[hh] * jnp.exp(gcols[hh])
            w_scr[rows, cs] = (kg + _dot(nb, kg.astype(BF16))).astype(BF16)
        return carry

    lax.fori_loop(0, n_chunks, prep_step, 0)

    def state_step(ci, carry):
        r0 = pl.multiple_of(ci * chunk, chunk)
        rows = pl.ds(r0, chunk)
        dl = dl_scr[pl.ds(ci, 1), :]
        for hh in range(hb):
            cs = slice(hh * dk, (hh + 1) * dk)
            st = s_ref[0, hh]
            stb = st.astype(BF16)
            vnb = (u_scr[rows, cs] - _dot(w_scr[rows, cs], stb)).astype(BF16)
            o = _dot(qg_scr[rows, cs], stb) + _dot(qk_scr[hh, rows, :], vnb)
            s_ref[0, hh] = st * dl[:, hh:hh + 1] + _tn(kd_scr[rows, cs], vnb)
            on = o * lax.rsqrt(jnp.mean(o * o, axis=-1, keepdims=True) + EPS) * gnorm
            zg = zg_ref[rows, cs]
            o_ref[rows, cs] = (on * (zg * jax.nn.sigmoid(zg))).astype(o_ref.dtype)
        return carry

    lax.fori_loop(0, n_chunks, state_step, 0)


def _gdn_sample_body(qkv_ref, qkt_ref, ab_ref, par_ref, zg_ref, gn_ref, st_ref, o_ref, so_ref, *, gh, hb, dk):
    ab = ab_ref[0]
    par = par_ref[...]
    g = -jnp.exp(par[0:1, :]) * _softplus(ab + par[1:2, :])
    dec = jnp.exp(g)
    beta = jax.nn.sigmoid(ab)
    qkt = qkt_ref[0]
    gnorm = gn_ref[...]
    for h in range(gh):
        la = (h // hb) * LANES + h % hb
        lb = la + 64
        st = st_ref[0, h] * dec[:, la:la + 1]
        kcol = qkt[:, gh + h:gh + h + 1]
        qcol = qkt[:, h:h + 1]
        v = qkv_ref[0, 2 * gh + h:2 * gh + h + 1, :]
        kst = jnp.sum(st * kcol, axis=0, keepdims=True)
        delta = (v - kst) * beta[:, lb:lb + 1]
        st = st + kcol * delta
        so_ref[0, h] = st
        o = jnp.sum(st * qcol, axis=0, keepdims=True)
        on = o * lax.rsqrt(jnp.mean(o * o, axis=-1, keepdims=True) + EPS) * gnorm
        zg = zg_ref[0, :, h * dk:(h + 1) * dk]
        o_ref[0, :, h * dk:(h + 1) * dk] = (on * (zg * jax.nn.sigmoid(zg))).astype(o_ref.dtype)


def _mla_q_body(c_ref, g_ref, w_ref, gq_ref, cos_ref, sin_ref, o_ref, *, mh, qk_dim, scale):
    x = c_ref[...]
    xn = x * lax.rsqrt(jnp.mean(x * x, axis=-1, keepdims=True) + EPS) * g_ref[...]
    q = _dot(xn.astype(BF16), w_ref[...])
    cos = cos_ref[...]
    sin = sin_ref[...]
    lane = lax.broadcasted_iota(jnp.int32, cos.shape, 1)
    gq = gq_ref[...]
    for h in range(mh):
        qn = q[:, h * 256:h * 256 + 128]
        qr = q[:, h * 256 + 128:(h + 1) * 256]
        ss = jnp.sum(qn * qn, axis=-1, keepdims=True) + jnp.sum(qr * qr, axis=-1, keepdims=True)
        rinv = lax.rsqrt(ss * (1.0 / qk_dim) + EPS) * scale
        y = qr * rinv * gq[:, 128:]
        o_ref[:, h * 256:h * 256 + 128] = (qn * rinv * gq[:, :128]).astype(o_ref.dtype)
        o_ref[:, h * 256 + 128:(h + 1) * 256] = (y * cos + _swap_halves64(y, lane) * sin).astype(o_ref.dtype)


def _mla_k_body(c_ref, kr_ref, g_ref, wk_ref, wv_ref, gk_ref, cos_ref, sin_ref, lat_ref, k_ref, v_ref,
                *, mh, qk_dim):
    x = c_ref[...]
    c = x * lax.rsqrt(jnp.mean(x * x, axis=-1, keepdims=True) + EPS) * g_ref[...]
    lat_ref[...] = c
    cb = c.astype(BF16)
    kn = _dot(cb, wk_ref[...])
    v_ref[...] = _dot(cb, wv_ref[...]).astype(v_ref.dtype)
    kr = kr_ref[...]
    ssr = jnp.sum(kr * kr, axis=-1, keepdims=True)
    gk = gk_ref[...]
    cos = cos_ref[...]
    sin = sin_ref[...]
    lane = lax.broadcasted_iota(jnp.int32, cos.shape, 1)
    y = kr * gk[:, 128:]
    krot = y * cos + _swap_halves64(y, lane) * sin
    for h in range(mh):
        knh = kn[:, h * 128:(h + 1) * 128]
        rinv = lax.rsqrt((jnp.sum(knh * knh, axis=-1, keepdims=True) + ssr) * (1.0 / qk_dim) + EPS)
        k_ref[:, h * 256:h * 256 + 128] = (knh * rinv * gk[:, :128]).astype(k_ref.dtype)
        k_ref[:, h * 256 + 128:(h + 1) * 256] = (krot * rinv).astype(k_ref.dtype)


def _causal_attn_body(q_ref, k_ref, v_ref, o_ref, *, tq):
    seq = q_ref.shape[0]
    for qi in range(seq // tq):
        kv = (qi + 1) * tq
        rows = slice(qi * tq, kv)
        s = _nt(q_ref[rows, :], k_ref[0:kv, :])
        qpos = qi * tq + lax.broadcasted_iota(jnp.int32, s.shape, 0)
        kpos = lax.broadcasted_iota(jnp.int32, s.shape, 1)
        s = jnp.where(kpos <= qpos, s, -jnp.inf)
        p = jnp.exp(s - jnp.max(s, axis=-1, keepdims=True))
        l = jnp.sum(p, axis=-1, keepdims=True)
        o_ref[rows, :] = (_dot(p.astype(BF16), v_ref[0:kv, :]) / l).astype(o_ref.dtype)


def _sample_prep_body(q_ref, k_ref, wukt_ref, gk_ref, qabs_ref, qab_ref, ss_ref, *, mh, kvr):
    gk = gk_ref[...]
    db = q_ref.shape[0]
    lane = lax.broadcasted_iota(jnp.int32, (db, LANES), 1)
    sgn = jnp.where(lane < 32, 1.0, -1.0)
    cols = []
    for h in range(mh):
        qh = q_ref[:, h * 256:(h + 1) * 256].astype(F32)
        kh = k_ref[:, h * 256:(h + 1) * 256].astype(F32)
        cols.append(jnp.sum(qh * kh, axis=-1, keepdims=True))
        qn = qh[:, :128] * gk[:, :128]
        qabs = _dot(qn, wukt_ref[h], precision=lax.Precision.HIGHEST)
        qabs_ref[:, h * kvr:(h + 1) * kvr] = qabs.astype(qabs_ref.dtype)
        qraw = qh[:, 128:]
        qa = qraw * gk[:, 128:]
        qb = _swap_halves64(qraw, lane) * gk[:, 128:] * sgn
        qab_ref[:, h * 128:(h + 1) * 128] = (qa + pltpu.roll(qb, 64, 1)).astype(qab_ref.dtype)
    ss_ref[...] = jnp.concatenate(cols, axis=-1)


def _sample_attn_body(pt_ref, qabs_ref, qab_ref, ss_ref, cnew_ref, wukt_ref, ctab_ref, stab_ref, lat_hbm, krt_hbm,
                      o_ref, m_scr, l_scr, acc_scr, lat_buf, krt_buf, sem,
                      *, npp, grp, mchunk, mh, nope, qk_dim, page):
    g = pl.program_id(1)
    n_groups = pl.num_programs(1)
    step = pl.program_id(0) * n_groups + g
    n_steps = pl.num_programs(0) * n_groups
    slot = step % 2

    def page_copies(t, sl):
        cps = []
        for r in range(npp):
            pg = pt_ref[t * npp + r]
            cps.append(pltpu.make_async_copy(lat_hbm.at[pg], lat_buf.at[sl, r], sem.at[0, sl]))
            cps.append(pltpu.make_async_copy(krt_hbm.at[pg], krt_buf.at[sl, r], sem.at[1, sl]))
        return cps

    @pl.when(step == 0)
    def _():
        for cp in page_copies(0, 0):
            cp.start()

    @pl.when(step + 1 < n_steps)
    def _():
        for cp in page_copies(step + 1, 1 - slot):
            cp.start()

    for cp in page_copies(step, slot):
        cp.wait()

    @pl.when(g == 0)
    def _():
        m_scr[...] = ss_ref[0]
        l_scr[...] = jnp.ones(l_scr.shape, F32)
        acc_scr[...] = jnp.broadcast_to(cnew_ref[0], acc_scr.shape)

    qabs = qabs_ref[0]
    qab = qab_ref[0]
    n = grp * page
    hpc = mchunk // nope
    s_parts = []
    cgs = []
    for sub in range(npp // grp):
        ids = range(sub * grp, (sub + 1) * grp)
        cg = jnp.concatenate([lat_buf[slot, r].astype(BF16) for r in ids], axis=0)
        krt = jnp.concatenate([krt_buf[slot, r] for r in ids], axis=1)
        cols = pl.ds(sub * n, n)
        ssr = jnp.sum(krt * krt, axis=0, keepdims=True)
        rhs_rope = jnp.concatenate([krt * ctab_ref[:, cols], krt * stab_ref[:, cols]], axis=0).astype(BF16)
        sr = _dot(qab, rhs_rope)
        ss_parts = []
        for mc in range(mh * nope // mchunk):
            kt = _nt(wukt_ref[mc * mchunk:(mc + 1) * mchunk, :], cg)
            ss_parts.append(jnp.sum((kt * kt).reshape(hpc, nope, n), axis=1))
        ss = jnp.concatenate(ss_parts, axis=0) if len(ss_parts) > 1 else ss_parts[0]
        rinv = lax.rsqrt((ss + ssr) * (1.0 / qk_dim) + EPS)
        s_parts.append((_nt(qabs, cg) + sr) * rinv)
        cgs.append(cg)
    s = jnp.concatenate(s_parts, axis=1) if len(s_parts) > 1 else s_parts[0]
    m_old = m_scr[...]
    m_new = jnp.maximum(m_old, jnp.max(s, axis=-1, keepdims=True))
    corr = jnp.exp(m_old - m_new)
    p = jnp.exp(s - m_new)
    l_scr[...] = l_scr[...] * corr + jnp.sum(p, axis=-1, keepdims=True)
    acc = acc_scr[...] * corr
    for sub, cg in enumerate(cgs):
        acc = acc + _dot(p[:, sub * n:(sub + 1) * n].astype(BF16), cg)
    acc_scr[...] = acc
    m_scr[...] = m_new

    @pl.when(g == pl.num_programs(1) - 1)
    def _():
        o_ref[0] = acc_scr[...] / l_scr[...]


def _head_mm_body(x_ref, w_ref, o_ref):
    o_ref[...] = _dot(x_ref[...].astype(BF16), w_ref[0]).astype(o_ref.dtype)


def _merge_body(og_ref, om_ref, wg_ref, wm_ref, ga_ref, gb_ref, o_ref):
    ya = _dot(og_ref[...], wg_ref[...])
    yb = _dot(om_ref[...], wm_ref[...])
    o_ref[...] = (jax.nn.sigmoid(ga_ref[...]) * ya + jax.nn.sigmoid(gb_ref[...]) * yb).astype(o_ref.dtype)


def _peer_select_body(q_ref, sk_ref, s1_ref, s2_ref, st_ref, *, topk):
    q = q_ref[...].astype(BF16)
    sk = sk_ref[0].astype(BF16)
    s1 = _nt(sk[0], q[:, :128])
    s2 = _nt(sk[1], q[:, 128:])

    def top_values(s):
        vals = []
        for _ in range(topk):
            m = jnp.max(s, axis=0, keepdims=True)
            vals.append(m)
            s = jnp.where(s >= m, -jnp.inf, s)
        return vals

    a = top_values(s1)
    b = jnp.concatenate(top_values(s2), axis=0)
    cand = jnp.concatenate([a[r] + b for r in range(topk)], axis=0)
    best = top_values(cand)
    z = jnp.exp(best[0] - best[0])
    for r in range(1, topk):
        z = z + jnp.exp(best[r] - best[0])
    s1_ref[0] = s1
    s2_ref[0] = s2
    ts = s1.shape[1]
    st_ref[0] = jnp.concatenate([best[topk - 1], best[0] + jnp.log(z), a[0], jnp.zeros((5, ts), F32)], axis=0)


def _peer_dense_body(x_ref, u_ref, v_ref, s1_ref, s2_ref, st_ref, o_ref, p_scr, wt_scr, e2_scr, *, ph, te, n_eblocks):
    j = pl.program_id(1)
    rd = (j + 1) % 2
    wr = j % 2

    @pl.when(j == 0)
    def _():
        o_ref[...] = jnp.zeros(o_ref.shape, F32)
        p_scr[1] = jnp.zeros(p_scr.shape[1:], p_scr.dtype)
        for h in range(ph):
            e2_scr[h] = jnp.exp(s2_ref[h] - (st_ref[h, 1:2, :] - st_ref[h, 2:3, :]))

    tm, d = x_ref.shape
    na = te // LANES
    nlc = tm // LANES
    kc = d // nlc
    jc = jnp.minimum(j, n_eblocks - 1)
    s1_rows = [[s1_ref[h, pl.ds(jc * na + a, 1), :] for h in range(ph)] for a in range(na)]
    e1_rows = [[jnp.exp(s1_rows[a][h] - st_ref[h, 2:3, :]) for h in range(ph)] for a in range(na)]

    ht = None
    for lc in range(nlc):
        ks = slice(lc * kc, (lc + 1) * kc)
        part = _nt(u_ref[:, ks], x_ref[:, ks])
        ht = part if ht is None else ht + part
        ls = slice(lc * LANES, (lc + 1) * LANES)
        for half in range(2):
            ks2 = slice(half * 64, (half + 1) * 64)
            wts = [None] * na
            for h in range(ph):
                s2h = s2_ref[h, ks2, ls]
                e2h = e2_scr[h, ks2, ls]
                tau = st_ref[h, 0:1, ls]
                for a in range(na):
                    term = jnp.where(s2h + s1_rows[a][h][:, ls] >= tau, e2h, 0.0) * e1_rows[a][h][:, ls]
                    wts[a] = term if wts[a] is None else wts[a] + term
            for a in range(na):
                wt_scr[a * LANES + half * 64:a * LANES + (half + 1) * 64, ls] = wts[a]

    p_old = p_scr[rd]
    for lc in range(nlc):
        cs = slice(lc * kc, (lc + 1) * kc)
        o_ref[:, cs] += _dot(p_old, v_ref[:, cs])
        ls = slice(lc * LANES, (lc + 1) * LANES)
        hta = ht[:, ls]
        blk = wt_scr[:, ls] * (0.5 * hta * (1.0 + lax.erf(hta * (2.0 ** -0.5))))
        p_scr[wr, ls, :] = blk.T.astype(p_scr.dtype)


def _peer_dense(hf, u, v, s1t, s2t, stat, row0, nrows, tm, te):
    d = hf.shape[1]
    n_exp = u.shape[0]
    ph, nkeys, _ = s1t.shape
    nb = n_exp // te
    blk0 = row0 // tm
    assert row0 % tm == 0 and nrows % tm == 0
    return pl.pallas_call(
        functools.partial(_peer_dense_body, ph=ph, te=te, n_eblocks=nb),
        grid=(nrows // tm, nb + 1),
        in_specs=[pl.BlockSpec((tm, d), lambda i, j: (i + blk0, 0)),
                  pl.BlockSpec((te, d), lambda i, j: (jnp.minimum(j, nb - 1), 0)),
                  pl.BlockSpec((te, d), lambda i, j: (jnp.maximum(j - 1, 0), 0)),
                  pl.BlockSpec((ph, nkeys, tm), lambda i, j: (0, 0, i + blk0)),
                  pl.BlockSpec((ph, nkeys, tm), lambda i, j: (0, 0, i + blk0)),
                  pl.BlockSpec((ph, 8, tm), lambda i, j: (0, 0, i + blk0))],
        out_specs=pl.BlockSpec((tm, d), lambda i, j: (i, 0)),
        out_shape=jax.ShapeDtypeStruct((nrows, d), F32),
        scratch_shapes=[pltpu.VMEM((2, tm, te), BF16), pltpu.VMEM((te, tm), F32), pltpu.VMEM((ph, nkeys, tm), F32)],
        compiler_params=_cp(("parallel", "arbitrary")),
        name="peer_dense",
    )(hf, u, v, s1t, s2t, stat)


def _add_body(a_ref, b_ref, o_ref):
    o_ref[...] = a_ref[...] + b_ref[...]


def _add_rows(a, b, row0, nrows, tm):
    d = a.shape[1]
    blk0 = row0 // tm
    return pl.pallas_call(
        _add_body,
        grid=(nrows // tm,),
        in_specs=[pl.BlockSpec((tm, d), lambda i: (i + blk0, 0)),
                  pl.BlockSpec((tm, d), lambda i: (i, 0))],
        out_specs=pl.BlockSpec((tm, d), lambda i: (i, 0)),
        out_shape=jax.ShapeDtypeStruct((nrows, d), F32),
        compiler_params=_cp(("parallel",)),
        name="residual_add",
    )(a, b)


def _rope_tables(pos, rope):
    half = rope // 2
    inv = jnp.power(ROPE_THETA, -2.0 * jnp.arange(half, dtype=F32) / rope)
    ang = pos.astype(F32)[:, None] * inv[None, :]
    return jnp.cos(ang), jnp.sin(ang)


def kernel(x_prompt, x_sample, cache_latent, cache_krope, page_table, state_gdn, state_conv, norm_mix_g, w_in,
           gdn_conv_w, gdn_a_log, gdn_dt_bias, gdn_norm_g, gdn_out, mla_q_norm_g, mla_kv_norm_g, mla_w_uq,
           mla_w_uk, mla_w_uv, mla_qk_norm_q, mla_qk_norm_k, mla_out, w_o, norm_ffn_g, peer_w_q, peer_subkeys,
           peer_u, peer_v):
    depth = w_in.shape[0]
    assert depth == 1, "single-layer trunk"
    bsz, seq, d = x_prompt.shape
    db, dseq, _ = x_sample.shape
    assert dseq == 1
    tp = bsz * seq
    t = tp + db
    page = cache_latent.shape[2]
    kvr = cache_latent.shape[3]
    rope = cache_krope.shape[3]
    n_pages = page_table.shape[1]
    past_len = n_pages * page
    gh, dk, dv = state_gdn.shape[2:]
    conv = gdn_conv_w.shape[1]
    cw = gdn_conv_w.shape[2]
    qkw = gh * dk
    vw = gh * dv
    qr = mla_q_norm_g.shape[1]
    mh, nope = mla_w_uk.shape[2:]
    vdim = mla_w_uv.shape[3]
    qk_dim = nope + rope
    ph, _, nkeys, qhalf = peer_subkeys.shape[1:]
    n_exp = peer_u.shape[1]
    assert dk == LANES and dv == LANES and nope == LANES and vdim == LANES and rope == 64
    assert nkeys == LANES and qhalf == LANES and page == LANES and n_exp == nkeys * nkeys
    assert tp % db == 0 and seq % GDN_CHUNK == 0

    hb = _pick(gh, (4, 2, 1))
    ngrp = gh // hb
    assert hb <= 64

    sizes = (cw, gh, gh, vw, qr, kvr, rope, d, d)
    offs = [0]
    for s in sizes:
        offs.append(offs[-1] + s)
    w_in0 = w_in[0]

    def wcol(i):
        return w_in0[:, offs[i]:offs[i + 1]]

    w_big = jnp.concatenate([wcol(0), wcol(3), wcol(7), wcol(8)], axis=1).astype(BF16)
    off_zg, off_ga, off_gb = cw, cw + vw, cw + vw + d
    wa = wcol(1).reshape(d, ngrp, hb)
    wb = wcol(2).reshape(d, ngrp, hb)
    w_ab = jnp.concatenate([wa, jnp.zeros((d, ngrp, 64 - hb), F32), wb, jnp.zeros((d, ngrp, 64 - hb), F32)],
                           axis=2).reshape(d, ngrp * LANES)
    w_small = jnp.concatenate([wcol(4), wcol(5), wcol(6), jnp.zeros((d, LANES - rope), F32), w_ab],
                              axis=1).astype(BF16)

    def group_lanes(vec):
        return jnp.pad(vec.reshape(ngrp, hb), ((0, 0), (0, LANES - hb))).reshape(ngrp * LANES)

    par = jnp.stack([group_lanes(gdn_a_log[0]), group_lanes(gdn_dt_bias[0])], axis=0)

    xp2 = x_prompt.reshape(tp, d)
    xs2 = x_sample.reshape(db, d)
    (xn,) = _rmsnorm_rows2(xp2, xs2, norm_mix_g[0])
    z_big = _matmul(xn, w_big, F32, name="in_proj_big")
    c_q, c_kv_raw, kr_pad, ab = _proj_multi(xn, w_small, (qr, kvr, LANES, ngrp * LANES))

    cwb = _pick(qkw, (512, 256, 128))
    nq_blocks = qkw // cwb
    conv_w = gdn_conv_w[0]
    qkv_act_p, cv_p = pl.pallas_call(
        functools.partial(_conv_prompt_body, nq_blocks=nq_blocks, dk=dk, conv=conv),
        grid=(bsz, cw // cwb),
        in_specs=[pl.BlockSpec((seq, cwb), lambda b, c: (b, c)),
                  pl.BlockSpec((conv, cwb), lambda b, c: (0, c))],
        out_specs=[pl.BlockSpec((seq, cwb), lambda b, c: (b, c)),
                   pl.BlockSpec((1, conv - 1, cwb), lambda b, c: (b, 0, c))],
        out_shape=[jax.ShapeDtypeStruct((tp, cw), BF16),
                   jax.ShapeDtypeStruct((bsz, conv - 1, cw), F32)],
        compiler_params=_cp(("parallel", "parallel")),
        name="gdn_conv_prompt",
    )(z_big, conv_w)

    st_conv2 = state_conv[0].reshape(db, (conv - 1) * cw)
    ncb = cw // cwb
    srow = tp // db
    qkv_act_s = pl.pallas_call(
        functools.partial(_conv_sample_body, nq_blocks=nq_blocks, dk=dk, conv=conv),
        grid=(ncb,),
        in_specs=[pl.BlockSpec((db, cwb), functools.partial(lambda c, j: (0, j * ncb + c), j=j))
                  for j in range(conv - 1)]
                 + [pl.BlockSpec((db, cwb), lambda c: (srow, c)),
                    pl.BlockSpec((conv, cwb), lambda c: (0, c))],
        out_specs=pl.BlockSpec((db, cwb), lambda c: (0, c)),
        out_shape=jax.ShapeDtypeStruct((db, cw), F32),
        compiler_params=_cp(("parallel",)),
        name="gdn_conv_sample",
    )(*([st_conv2] * (conv - 1)), z_big, conv_w)

    hw = hb * dk
    assert off_zg % hw == 0 and qkw % hw == 0
    par3 = par.reshape(2, ngrp, LANES).transpose(1, 0, 2)
    gnorm = gdn_norm_g[0].reshape(1, dv)
    o_gdn_p, s_gdn_p = pl.pallas_call(
        functools.partial(_gdn_prompt_body, hb=hb, chunk=GDN_CHUNK, dk=dk),
        grid=(bsz, ngrp),
        in_specs=[pl.BlockSpec((seq, hw), lambda b, g: (b, g)),
                  pl.BlockSpec((seq, hw), lambda b, g: (b, ngrp + g)),
                  pl.BlockSpec((seq, hw), lambda b, g: (b, 2 * ngrp + g)),
                  pl.BlockSpec((seq, hw), lambda b, g: (b, off_zg // hw + g)),
                  pl.BlockSpec((seq, LANES), lambda b, g: (b, g)),
                  pl.BlockSpec((1, 2, LANES), lambda b, g: (g, 0, 0)),
                  pl.BlockSpec((1, dv), lambda b, g: (0, 0))],
        out_specs=[pl.BlockSpec((seq, hw), lambda b, g: (b, g)),
                   pl.BlockSpec((1, hb, dk, dv), lambda b, g: (b, g, 0, 0))],
        out_shape=[jax.ShapeDtypeStruct((tp, vw), BF16),
                   jax.ShapeDtypeStruct((bsz, gh, dk, dv), F32)],
        scratch_shapes=[pltpu.VMEM((seq, hw), F32), pltpu.VMEM((seq, hw), BF16), pltpu.VMEM((seq, hw), BF16),
                        pltpu.VMEM((seq, hw), BF16), pltpu.VMEM((hb, seq, GDN_CHUNK), BF16),
                        pltpu.VMEM((seq // GDN_CHUNK, LANES), F32)],
        compiler_params=_cp(("parallel", "parallel")),
        name="gdn_prompt",
    )(qkv_act_p, qkv_act_p, qkv_act_p, z_big, ab, par3, gnorm)

    qkv_s3 = qkv_act_s.reshape(db, 3 * gh, dk)
    qkt_s = jnp.swapaxes(qkv_s3[:, :2 * gh, :], 1, 2)
    ab_s = ab[tp:].reshape(db, 1, ngrp * LANES)
    zg_s = z_big[tp:, off_zg:off_zg + vw].reshape(db, 1, vw)
    o_gdn_s, s_gdn_s = pl.pallas_call(
        functools.partial(_gdn_sample_body, gh=gh, hb=hb, dk=dk),
        grid=(db,),
        in_specs=[pl.BlockSpec((1, 3 * gh, dk), lambda b: (b, 0, 0)),
                  pl.BlockSpec((1, dk, 2 * gh), lambda b: (b, 0, 0)),
                  pl.BlockSpec((1, 1, ngrp * LANES), lambda b: (b, 0, 0)),
                  pl.BlockSpec((2, ngrp * LANES), lambda b: (0, 0)),
                  pl.BlockSpec((1, 1, vw), lambda b: (b, 0, 0)),
                  pl.BlockSpec((1, dv), lambda b: (0, 0)),
                  pl.BlockSpec((1, gh, dk, dv), lambda b: (b, 0, 0, 0))],
        out_specs=[pl.BlockSpec((1, 1, vw), lambda b: (b, 0, 0)),
                   pl.BlockSpec((1, gh, dk, dv), lambda b: (b, 0, 0, 0))],
        out_shape=[jax.ShapeDtypeStruct((db, 1, vw), BF16),
                   jax.ShapeDtypeStruct((db, gh, dk, dv), F32)],
        compiler_params=_cp(("parallel",)),
        name="gdn_sample",
    )(qkv_s3, qkt_s, ab_s, par, zg_s, gnorm, state_gdn[0].astype(F32))
    o_gdn = jnp.concatenate([o_gdn_p, o_gdn_s.reshape(db, vw)], axis=0)

    pos_all = jnp.concatenate([jnp.tile(jnp.arange(seq, dtype=jnp.int32), bsz),
                               jnp.full((db,), past_len, jnp.int32)])
    cos_h, sin_h = _rope_tables(pos_all, rope)
    zpad = jnp.zeros((t, LANES - rope), F32)
    cos_t = jnp.concatenate([cos_h, cos_h, zpad], axis=1)
    sin_t = jnp.concatenate([-sin_h, sin_h, zpad], axis=1)

    w_uq3 = mla_w_uq[0].reshape(qr, mh, qk_dim)
    w_uq_pad = jnp.concatenate([w_uq3, jnp.zeros((qr, mh, 256 - qk_dim), F32)], axis=2).reshape(qr, mh * 256)
    w_uq_pad = w_uq_pad.astype(BF16)

    def gain256(gvec):
        return jnp.concatenate([gvec, jnp.zeros((256 - qk_dim,), F32)]).reshape(1, 256)

    tmq = _pick(t, (320, 256, 128, 64, 8))
    sm_scale = qk_dim ** -0.5
    q_all = pl.pallas_call(
        functools.partial(_mla_q_body, mh=mh, qk_dim=qk_dim, scale=sm_scale),
        grid=(t // tmq,),
        in_specs=[pl.BlockSpec((tmq, qr), lambda i: (i, 0)),
                  pl.BlockSpec((1, qr), lambda i: (0, 0)),
                  pl.BlockSpec((qr, mh * 256), lambda i: (0, 0)),
                  pl.BlockSpec((1, 256), lambda i: (0, 0)),
                  pl.BlockSpec((tmq, LANES), lambda i: (i, 0)),
                  pl.BlockSpec((tmq, LANES), lambda i: (i, 0))],
        out_specs=pl.BlockSpec((tmq, mh * 256), lambda i: (i, 0)),
        out_shape=jax.ShapeDtypeStruct((t, mh * 256), BF16),
        compiler_params=_cp(("parallel",)),
        name="mla_q",
    )(c_q, mla_q_norm_g[0].reshape(1, qr), w_uq_pad, gain256(mla_qk_norm_q[0]), cos_t, sin_t)

    w_uk2 = mla_w_uk[0].reshape(kvr, mh * nope).astype(BF16)
    w_uv2 = mla_w_uv[0].reshape(kvr, mh * vdim).astype(BF16)
    gk256 = gain256(mla_qk_norm_k[0])
    lat_all, k_all, v_all = pl.pallas_call(
        functools.partial(_mla_k_body, mh=mh, qk_dim=qk_dim),
        grid=(t // tmq,),
        in_specs=[pl.BlockSpec((tmq, kvr), lambda i: (i, 0)),
                  pl.BlockSpec((tmq, LANES), lambda i: (i, 0)),
                  pl.BlockSpec((1, kvr), lambda i: (0, 0)),
                  pl.BlockSpec((kvr, mh * nope), lambda i: (0, 0)),
                  pl.BlockSpec((kvr, mh * vdim), lambda i: (0, 0)),
                  pl.BlockSpec((1, 256), lambda i: (0, 0)),
                  pl.BlockSpec((tmq, LANES), lambda i: (i, 0)),
                  pl.BlockSpec((tmq, LANES), lambda i: (i, 0))],
        out_specs=[pl.BlockSpec((tmq, kvr), lambda i: (i, 0)),
                   pl.BlockSpec((tmq, mh * 256), lambda i: (i, 0)),
                   pl.BlockSpec((tmq, mh * vdim), lambda i: (i, 0))],
        out_shape=[jax.ShapeDtypeStruct((t, kvr), F32),
                   jax.ShapeDtypeStruct((t, mh * 256), BF16),
                   jax.ShapeDtypeStruct((t, mh * vdim), BF16)],
        compiler_params=_cp(("parallel",)),
        name="mla_k",
    )(c_kv_raw, kr_pad, mla_kv_norm_g[0].reshape(1, kvr), w_uk2, w_uv2, gk256, cos_t, sin_t)

    tq = _pick(seq, (512, 256, 128))
    o_mla_p = pl.pallas_call(
        functools.partial(_causal_attn_body, tq=tq),
        grid=(bsz, mh),
        in_specs=[pl.BlockSpec((seq, 256), lambda b, h: (b, h)),
                  pl.BlockSpec((seq, 256), lambda b, h: (b, h)),
                  pl.BlockSpec((seq, vdim), lambda b, h: (b, h))],
        out_specs=pl.BlockSpec((seq, vdim), lambda b, h: (b, h)),
        out_shape=jax.ShapeDtypeStruct((tp, mh * vdim), BF16),
        compiler_params=_cp(("parallel", "parallel")),
        name="mla_attn_prompt",
    )(q_all, k_all, v_all)

    w_ukt = jnp.transpose(mla_w_uk[0], (1, 2, 0))
    q_s = q_all[tp:]
    k_s = k_all[tp:]
    qabs, qab, s_self = pl.pallas_call(
        functools.partial(_sample_prep_body, mh=mh, kvr=kvr),
        out_shape=[jax.ShapeDtypeStruct((db, mh * kvr), BF16),
                   jax.ShapeDtypeStruct((db, mh * LANES), BF16),
                   jax.ShapeDtypeStruct((db, mh), F32)],
        compiler_params=pltpu.CompilerParams(vmem_limit_bytes=VMEM_LIMIT),
        name="mla_sample_prep",
    )(q_s, k_s, w_ukt, gk256)

    npp = _pick(n_pages, (16, 8, 4, 2, 1))
    grp = _pick(npp, (4, 2, 1))
    mchunk = _pick(mh * nope, (512, 256, 128))
    pos_past = jnp.arange(past_len, dtype=jnp.int32)
    cos_p, sin_p = _rope_tables(pos_past, rope)
    ctab = jnp.concatenate([cos_p, cos_p], axis=1).T
    stab = jnp.concatenate([sin_p, sin_p], axis=1).T
    w_ukt2 = w_ukt.reshape(mh * nope, kvr).astype(BF16)
    lat0 = cache_latent[0]
    krt0 = jnp.swapaxes(cache_krope[0], 1, 2)

    lat_s = pl.pallas_call(
        functools.partial(_sample_attn_body, npp=npp, grp=grp, mchunk=mchunk, mh=mh, nope=nope, qk_dim=qk_dim,
                          page=page),
        grid_spec=pltpu.PrefetchScalarGridSpec(
            num_scalar_prefetch=1,
            grid=(db, n_pages // npp),
            in_specs=[pl.BlockSpec((1, mh, kvr), lambda b, g, pt: (b, 0, 0)),
                      pl.BlockSpec((1, mh, LANES), lambda b, g, pt: (b, 0, 0)),
                      pl.BlockSpec((1, mh, 1), lambda b, g, pt: (b, 0, 0)),
                      pl.BlockSpec((1, 1, kvr), lambda b, g, pt: (b, 0, 0)),
                      pl.BlockSpec((mh * nope, kvr), lambda b, g, pt: (0, 0)),
                      pl.BlockSpec((rope, npp * page), lambda b, g, pt: (0, g)),
                      pl.BlockSpec((rope, npp * page), lambda b, g, pt: (0, g)),
                      pl.BlockSpec(memory_space=pl.ANY),
                      pl.BlockSpec(memory_space=pl.ANY)],
            out_specs=pl.BlockSpec((1, mh, kvr), lambda b, g, pt: (b, 0, 0)),
            scratch_shapes=[pltpu.VMEM((mh, 1), F32), pltpu.VMEM((mh, 1), F32), pltpu.VMEM((mh, kvr), F32),
                            pltpu.VMEM((2, npp, page, kvr), F32), pltpu.VMEM((2, npp, rope, page), F32),
                            pltpu.SemaphoreType.DMA((2, 2))],
        ),
        out_shape=jax.ShapeDtypeStruct((db, mh, kvr), F32),
        compiler_params=_cp(("arbitrary", "arbitrary")),
        name="mla_sample_attn",
    )(page_table.reshape(-1).astype(jnp.int32),
      qabs.reshape(db, mh, kvr), qab.reshape(db, mh, LANES), s_self.reshape(db, mh, 1),
      lat_all[tp:].reshape(db, 1, kvr), w_ukt2, ctab, stab, lat0, krt0)

    w_uv3 = jnp.transpose(mla_w_uv[0], (1, 0, 2)).astype(BF16)
    o_mla_s = pl.pallas_call(
        _head_mm_body,
        grid=(mh,),
        in_specs=[pl.BlockSpec((db, kvr), lambda h: (0, h)),
                  pl.BlockSpec((1, kvr, vdim), lambda h: (h, 0, 0))],
        out_specs=pl.BlockSpec((db, vdim), lambda h: (0, h)),
        out_shape=jax.ShapeDtypeStruct((db, mh * vdim), BF16),
        compiler_params=_cp(("parallel",)),
        name="mla_sample_uv",
    )(lat_s.reshape(db, mh * kvr), w_uv3)
    o_mla = jnp.concatenate([o_mla_p, o_mla_s], axis=0)

    tm = _pick(t, (640, 512, 256, 128, 64, 8))
    tn = _pick(d, (512, 256, 128))
    assert off_ga % tn == 0 and off_gb % tn == 0
    merged = pl.pallas_call(
        _merge_body,
        grid=(d // tn, t // tm),
        in_specs=[pl.BlockSpec((tm, vw), lambda j, i: (i, 0)),
                  pl.BlockSpec((tm, mh * vdim), lambda j, i: (i, 0)),
                  pl.BlockSpec((vw, tn), lambda j, i: (0, j)),
                  pl.BlockSpec((mh * vdim, tn), lambda j, i: (0, j)),
                  pl.BlockSpec((tm, tn), lambda j, i: (i, off_ga // tn + j)),
                  pl.BlockSpec((tm, tn), lambda j, i: (i, off_gb // tn + j))],
        out_specs=pl.BlockSpec((tm, tn), lambda j, i: (i, j)),
        out_shape=jax.ShapeDtypeStruct((t, d), BF16),
        compiler_params=_cp(("parallel", "parallel")),
        name="mixer_merge",
    )(o_gdn, o_mla, gdn_out[0].astype(BF16), mla_out[0].astype(BF16), z_big, z_big)
    y_o = _matmul(merged, w_o[0].astype(BF16), F32, name="out_proj")

    x2, hf = _rmsnorm_rows2(xp2, xs2, norm_ffn_g[0], branch=y_o)
    pq = _matmul(hf, peer_w_q[0].astype(BF16), F32, name="peer_query")
    ts = _pick(t, (128,))
    s1t, s2t, stat = pl.pallas_call(
        functools.partial(_peer_select_body, topk=PEER_TOPK),
        grid=(t // ts, ph),
        in_specs=[pl.BlockSpec((ts, 2 * qhalf), lambda i, h: (i, h)),
                  pl.BlockSpec((1, 2, nkeys, qhalf), lambda i, h: (h, 0, 0, 0))],
        out_specs=[pl.BlockSpec((1, nkeys, ts), lambda i, h: (h, 0, i)),
                   pl.BlockSpec((1, nkeys, ts), lambda i, h: (h, 0, i)),
                   pl.BlockSpec((1, 8, ts), lambda i, h: (h, 0, i))],
        out_shape=[jax.ShapeDtypeStruct((ph, nkeys, t), F32),
                   jax.ShapeDtypeStruct((ph, nkeys, t), F32),
                   jax.ShapeDtypeStruct((ph, 8, t), F32)],
        compiler_params=_cp(("parallel", "parallel")),
        name="peer_select",
    )(pq, peer_subkeys[0])

    peer_u16 = peer_u[0].astype(BF16)
    peer_v16 = peer_v[0].astype(BF16)
    te = _pick(n_exp, (512, 256, 128))
    tmp = _pick(tp, (512, 256, 128))
    tms = _pick(db, (128,))
    assert tp % tms == 0
    peer_p = _peer_dense(hf, peer_u16, peer_v16, s1t, s2t, stat, 0, tp, tmp, te)
    peer_s = _peer_dense(hf, peer_u16, peer_v16, s1t, s2t, stat, tp, db, tms, te)

    tma = _pick(db, (128, 64, 8))
    assert tp % tma == 0
    y_p = _add_rows(x2, peer_p, 0, tp, tma).reshape(bsz, seq, d)
    y_s = _add_rows(x2, peer_s, tp, db, tma).reshape(db, 1, d)

    lat_p = lat_all[:tp].reshape(1, bsz, seq, kvr)
    lat_sn = lat_all[tp:].reshape(1, db, 1, kvr)
    kr_p = kr_pad[:tp, :rope].reshape(1, bsz, seq, rope)
    kr_s = kr_pad[tp:, :rope].reshape(1, db, 1, rope)
    cv_p = cv_p[None]
    cv_s = jnp.concatenate([state_conv[0][:, 1:, :], z_big[tp:, :cw].reshape(db, 1, cw)], axis=1)[None]
    return (y_p, y_s, lat_p, kr_p, lat_sn, kr_s,
            s_gdn_p[None].astype(state_gdn.dtype), s_gdn_s[None].astype(state_gdn.dtype), cv_p, cv_s)
```

```python
import functools
import math

import jax
import jax.numpy as jnp
from jax import lax
from jax.experimental import pallas as pl
from jax.experimental.pallas import tpu as pltpu

F32 = jnp.float32
BF16 = jnp.bfloat16

EPS = 1e-6
ROPE_THETA = 10000.0
GDN_CHUNK = 64
PEER_TOPK = 16
LANES = 128
VMEM_LIMIT = 56 * 1024 * 1024


def _cp(dims, vmem=VMEM_LIMIT):
    return pltpu.CompilerParams(dimension_semantics=dims, vmem_limit_bytes=vmem)


def _pick(n, cands):
    for c in cands:
        if c <= n and n % c == 0:
            return c
    return n


def _nt(a, b, precision=None):
    return lax.dot_general(a, b, (((1,), (1,)), ((), ())),
                           preferred_element_type=F32, precision=precision)


def _tn(a, b):
    return lax.dot_general(a, b, (((0,), (0,)), ((), ())), preferred_element_type=F32)


def _dot(a, b, precision=None):
    return jnp.dot(a, b, preferred_element_type=F32, precision=precision)


def _softplus(x):
    return jnp.maximum(x, 0.0) + jnp.log1p(jnp.exp(-jnp.abs(x)))


def _swap_halves64(y, lane):
    return jnp.where(lane < 32, pltpu.roll(y, 96, 1), pltpu.roll(y, 32, 1))


def _rmsnorm2_body(xp_ref, xs_ref, g_ref, *rest, np_blocks, with_residual):
    if with_residual:
        y_ref, x2_ref, o_ref = rest
    else:
        (o_ref,) = rest
    i = pl.program_id(0)

    def emit(x):
        if with_residual:
            x = x + y_ref[...]
            x2_ref[...] = x
        y = x * lax.rsqrt(jnp.mean(x * x, axis=-1, keepdims=True) + EPS)
        o_ref[...] = (y * g_ref[...]).astype(o_ref.dtype)

    @pl.when(i < np_blocks)
    def _():
        emit(xp_ref[...])

    @pl.when(i >= np_blocks)
    def _():
        emit(xs_ref[...])


def _rmsnorm_rows2(xp, xs, g, branch=None):
    tp, d = xp.shape
    db = xs.shape[0]
    tm = _pick(db, (128, 64, 8))
    assert tp % tm == 0 and db % tm == 0
    npb = tp // tm
    t = tp + db
    row = pl.BlockSpec((tm, d), lambda i: (i, 0))
    in_specs = [pl.BlockSpec((tm, d), lambda i: (jnp.minimum(i, npb - 1), 0)),
                pl.BlockSpec((tm, d), lambda i: (jnp.maximum(i - npb, 0), 0)),
                pl.BlockSpec((1, d), lambda i: (0, 0))]
    args = [xp, xs, g.reshape(1, d)]
    out_specs = [row]
    out_shape = [jax.ShapeDtypeStruct((t, d), BF16)]
    if branch is not None:
        in_specs.append(row)
        args.append(branch)
        out_specs = [row, row]
        out_shape = [jax.ShapeDtypeStruct((t, d), F32)] + out_shape
    return pl.pallas_call(
        functools.partial(_rmsnorm2_body, np_blocks=npb, with_residual=branch is not None),
        grid=(t // tm,),
        in_specs=in_specs,
        out_specs=out_specs,
        out_shape=out_shape,
        compiler_params=_cp(("parallel",)),
        name="rmsnorm_rows",
    )(*args)


def _mm_body(x_ref, w_ref, o_ref):
    o_ref[...] = _dot(x_ref[...], w_ref[...]).astype(o_ref.dtype)


def _matmul(x, w, out_dtype, name="matmul"):
    m, k = x.shape
    n = w.shape[1]
    tm = _pick(m, (640, 512, 256, 128, 64, 8))
    tn = _pick(n, (512, 256, 128))
    return pl.pallas_call(
        _mm_body,
        grid=(n // tn, m // tm),
        in_specs=[pl.BlockSpec((tm, k), lambda j, i: (i, 0)),
                  pl.BlockSpec((k, tn), lambda j, i: (0, j))],
        out_specs=pl.BlockSpec((tm, tn), lambda j, i: (i, j)),
        out_shape=jax.ShapeDtypeStruct((m, n), out_dtype),
        compiler_params=_cp(("parallel", "parallel")),
        name=name,
    )(x, w)


def _proj_multi_body(x_ref, w_ref, *o_refs, widths):
    y = _dot(x_ref[...], w_ref[...])
    off = 0
    for o_ref, wd in zip(o_refs, widths):
        o_ref[...] = y[:, off:off + wd]
        off += wd


def _proj_multi(x, w, widths):
    m, k = x.shape
    n = w.shape[1]
    tm = _pick(m, (640, 512, 256, 128, 64, 8))
    return pl.pallas_call(
        functools.partial(_proj_multi_body, widths=widths),
        grid=(m // tm,),
        in_specs=[pl.BlockSpec((tm, k), lambda i: (i, 0)),
                  pl.BlockSpec((k, n), lambda i: (0, 0))],
        out_specs=[pl.BlockSpec((tm, wd), lambda i: (i, 0)) for wd in widths],
        out_shape=[jax.ShapeDtypeStruct((m, wd), F32) for wd in widths],
        compiler_params=_cp(("parallel",)),
        name="proj_multi",
    )(x, w)


def _conv_finish(acc, o_ref, c, nq_blocks, dk):
    y = acc * jax.nn.sigmoid(acc)
    cw = y.shape[1]

    def normed(scale):
        parts = []
        for s in range(cw // dk):
            seg = y[:, s * dk:(s + 1) * dk]
            parts.append(seg * (lax.rsqrt(jnp.sum(seg * seg, axis=-1, keepdims=True) + EPS) * scale))
        return jnp.concatenate(parts, axis=-1) if len(parts) > 1 else parts[0]

    @pl.when(c < nq_blocks)
    def _():
        o_ref[...] = normed(dk ** -0.5).astype(o_ref.dtype)

    @pl.when(jnp.logical_and(c >= nq_blocks, c < 2 * nq_blocks))
    def _():
        o_ref[...] = normed(1.0).astype(o_ref.dtype)

    @pl.when(c >= 2 * nq_blocks)
    def _():
        o_ref[...] = y.astype(o_ref.dtype)


def _conv_prompt_body(x_ref, w_ref, o_ref, tail_ref, *, nq_blocks, dk, conv):
    x = x_ref[...]
    w = w_ref[...]
    tail_ref[0] = x[x.shape[0] - (conv - 1):, :]
    row = lax.broadcasted_iota(jnp.int32, x.shape, 0)
    acc = None
    for j in range(conv):
        sh = conv - 1 - j
        xs = x if sh == 0 else jnp.where(row >= sh, pltpu.roll(x, sh, 0), 0.0)
        term = xs * w[j:j + 1, :]
        acc = term if acc is None else acc + term
    _conv_finish(acc, o_ref, pl.program_id(1), nq_blocks, dk)


def _conv_sample_body(*refs, nq_blocks, dk, conv):
    hist = refs[:conv - 1]
    x_ref, w_ref, o_ref = refs[conv - 1:]
    w = w_ref[...]
    acc = None
    for j in range(conv):
        src = x_ref[...] if j == conv - 1 else hist[j][...]
        term = src * w[j:j + 1, :]
        acc = term if acc is None else acc + term
    _conv_finish(acc, o_ref, pl.program_id(0), nq_blocks, dk)


def _gdn_prompt_body(q_ref, k_ref, v_ref, zg_ref, ab_ref, par_ref, gn_ref, o_ref, s_ref,
                     u_scr, w_scr, qg_scr, kd_scr, qk_scr, dl_scr, *, hb, chunk, dk):
    seq = q_ref.shape[0]
    n_chunks = seq // chunk
    s_ref[...] = jnp.zeros(s_ref.shape, F32)
    ii = lax.broadcasted_iota(jnp.int32, (chunk, chunk), 0)
    jj = lax.broadcasted_iota(jnp.int32, (chunk, chunk), 1)
    causal = ii >= jj
    strict = ii > jj
    rowi = lax.broadcasted_iota(jnp.int32, (chunk, LANES), 0)
    par = par_ref[0]
    a_log = par[0:1, :]
    dt_bias = par[1:2, :]
    gnorm = gn_ref[...]
    n_double = int(math.log2(chunk)) - 1

    def prep_step(ci, carry):
        r0 = pl.multiple_of(ci * chunk, chunk)
        rows = pl.ds(r0, chunk)
        ab = ab_ref[rows, :]
        g = -jnp.exp(a_log) * _softplus(ab + dt_bias)
        beta = jax.nn.sigmoid(ab)
        gc = g
        sh = 1
        while sh < chunk:
            gc = gc + jnp.where(rowi >= sh, pltpu.roll(gc, sh, 0), 0.0)
            sh *= 2
        gct = gc.T
        dl_scr[pl.ds(ci, 1), :] = jnp.exp(gc[chunk - 1:chunk, :])
        heads = range(hb)
        xs, kbs, vbs, gcols = [], [], [], []
        for hh in heads:
            cs = slice(hh * dk, (hh + 1) * dk)
            gcol = gc[:, hh:hh + 1]
            bcol = beta[:, 64 + hh:65 + hh]
            q = q_ref[rows, cs].astype(F32)
            k = k_ref[rows, cs].astype(F32)
            kbf = k_ref[rows, cs].astype(BF16)
            decay = jnp.exp(jnp.where(causal, gcol - gct[hh:hh + 1, :], -jnp.inf))
            kb = k * bcol
            xs.append(-jnp.where(strict, _nt(kb.astype(BF16), kbf) * decay, 0.0))
            qk = jnp.where(causal, _nt(q_ref[rows, cs].astype(BF16), kbf) * decay, 0.0)
            qk_scr[hh, rows, :] = qk.astype(BF16)
            g_last = gc[chunk - 1:chunk, hh:hh + 1]
            qg_scr[rows, cs] = (q * jnp.exp(gcol)).astype(BF16)
            kd_scr[rows, cs] = (k * jnp.exp(g_last - gcol)).astype(BF16)
            kbs.append(kb)
            vbs.append(v_ref[rows, cs].astype(F32) * bcol)
            gcols.append(gcol)
        ns = list(xs)
        ps = list(xs)
        for _ in range(n_double):
            pb = [p.astype(BF16) for p in ps]
            ps = [_dot(b_, b_) for b_ in pb]
            ns = [n_ + p_ + _dot(n_.astype(BF16), p_.astype(BF16)) for n_, p_ in zip(ns, ps)]
        for hh in heads:
            cs = slice(hh * dk, (hh + 1) * dk)
            nb = ns[hh].astype(BF16)
            u_scr[rows, cs] = vbs[hh] + _dot(nb, vbs[hh].astype(BF16))
            kg = kbs[hh] * jnp.exp(gcols[hh])
            w_scr[rows, cs] = (kg + _dot(nb, kg.astype(BF16))).astype(BF16)
        return carry

    lax.fori_loop(0, n_chunks, prep_step, 0)

    def state_step(ci, carry):
        r0 = pl.multiple_of(ci * chunk, chunk)
        rows = pl.ds(r0, chunk)
        dl = dl_scr[pl.ds(ci, 1), :]
        for hh in range(hb):
            cs = slice(hh * dk, (hh + 1) * dk)
            st = s_ref[0, hh]
            stb = st.astype(BF16)
            vnb = (u_scr[rows, cs] - _dot(w_scr[rows, cs], stb)).astype(BF16)
            o = _dot(qg_scr[rows, cs], stb) + _dot(qk_scr[hh, rows, :], vnb)
            s_ref[0, hh] = st * dl[:, hh:hh + 1] + _tn(kd_scr[rows, cs], vnb)
            on = o * lax.rsqrt(jnp.mean(o * o, axis=-1, keepdims=True) + EPS) * gnorm
            zg = zg_ref[rows, cs]
            o_ref[rows, cs] = (on * (zg * jax.nn.sigmoid(zg))).astype(o_ref.dtype)
        return carry

    lax.fori_loop(0, n_chunks, state_step, 0)


def _gdn_sample_body(qkv_ref, qkt_ref, ab_ref, par_ref, zg_ref, gn_ref, st_ref, o_ref, so_ref, *, gh, hb, dk):
    par = par_ref[...]
    gnorm = gn_ref[...]
    for bi in range(qkv_ref.shape[0]):
        ab = ab_ref[bi]
        g = -jnp.exp(par[0:1, :]) * _softplus(ab + par[1:2, :])
        dec = jnp.exp(g)
        beta = jax.nn.sigmoid(ab)
        qkt = qkt_ref[bi]
        for h in range(gh):
            la = (h // hb) * LANES + h % hb
            lb = la + 64
            st = st_ref[bi, h] * dec[:, la:la + 1]
            kcol = qkt[:, gh + h:gh + h + 1]
            qcol = qkt[:, h:h + 1]
            v = qkv_ref[bi, 2 * gh + h:2 * gh + h + 1, :]
            kst = jnp.sum(st * kcol, axis=0, keepdims=True)
            delta = (v - kst) * beta[:, lb:lb + 1]
            st = st + kcol * delta
            so_ref[bi, h] = st
            o = jnp.sum(st * qcol, axis=0, keepdims=True)
            on = o * lax.rsqrt(jnp.mean(o * o, axis=-1, keepdims=True) + EPS) * gnorm
            zg = zg_ref[bi, :, h * dk:(h + 1) * dk]
            o_ref[bi, :, h * dk:(h + 1) * dk] = (on * (zg * jax.nn.sigmoid(zg))).astype(o_ref.dtype)


def _mla_q_body(c_ref, g_ref, w_ref, gq_ref, cos_ref, sin_ref, o_ref, *, mh, qk_dim, scale):
    x = c_ref[...]
    xn = x * lax.rsqrt(jnp.mean(x * x, axis=-1, keepdims=True) + EPS) * g_ref[...]
    q = _dot(xn.astype(BF16), w_ref[...])
    cos = cos_ref[...]
    sin = sin_ref[...]
    lane = lax.broadcasted_iota(jnp.int32, cos.shape, 1)
    gq = gq_ref[...]
    for h in range(mh):
        qn = q[:, h * 256:h * 256 + 128]
        qr = q[:, h * 256 + 128:(h + 1) * 256]
        ss = jnp.sum(qn * qn, axis=-1, keepdims=True) + jnp.sum(qr * qr, axis=-1, keepdims=True)
        rinv = lax.rsqrt(ss * (1.0 / qk_dim) + EPS) * scale
        y = qr * rinv * gq[:, 128:]
        o_ref[:, h * 256:h * 256 + 128] = (qn * rinv * gq[:, :128]).astype(o_ref.dtype)
        o_ref[:, h * 256 + 128:(h + 1) * 256] = (y * cos + _swap_halves64(y, lane) * sin).astype(o_ref.dtype)


def _mla_k_body(c_ref, kr_ref, g_ref, wk_ref, wv_ref, gk_ref, cos_ref, sin_ref, lat_ref, k_ref, v_ref,
                *, mh, qk_dim):
    x = c_ref[...]
    c = x * lax.rsqrt(jnp.mean(x * x, axis=-1, keepdims=True) + EPS) * g_ref[...]
    lat_ref[...] = c
    cb = c.astype(BF16)
    kn = _dot(cb, wk_ref[...])
    v_ref[...] = _dot(cb, wv_ref[...]).astype(v_ref.dtype)
    kr = kr_ref[...]
    ssr = jnp.sum(kr * kr, axis=-1, keepdims=True)
    gk = gk_ref[...]
    cos = cos_ref[...]
    sin = sin_ref[...]
    lane = lax.broadcasted_iota(jnp.int32, cos.shape, 1)
    y = kr * gk[:, 128:]
    krot = y * cos + _swap_halves64(y, lane) * sin
    for h in range(mh):
        knh = kn[:, h * 128:(h + 1) * 128]
        rinv = lax.rsqrt((jnp.sum(knh * knh, axis=-1, keepdims=True) + ssr) * (1.0 / qk_dim) + EPS)
        k_ref[:, h * 256:h * 256 + 128] = (knh * rinv * gk[:, :128]).astype(k_ref.dtype)
        k_ref[:, h * 256 + 128:(h + 1) * 256] = (krot * rinv).astype(k_ref.dtype)


def _causal_attn_body(q_ref, k_ref, v_ref, o_ref, *, tq):
    seq = q_ref.shape[0]
    for qi in range(seq // tq):
        kv = (qi + 1) * tq
        rows = slice(qi * tq, kv)
        s = _nt(q_ref[rows, :], k_ref[0:kv, :])
        qpos = qi * tq + lax.broadcasted_iota(jnp.int32, s.shape, 0)
        kpos = lax.broadcasted_iota(jnp.int32, s.shape, 1)
        s = jnp.where(kpos <= qpos, s, -jnp.inf)
        p = jnp.exp(s - jnp.max(s, axis=-1, keepdims=True))
        l = jnp.sum(p, axis=-1, keepdims=True)
        o_ref[rows, :] = (_dot(p.astype(BF16), v_ref[0:kv, :]) / l).astype(o_ref.dtype)


def _sample_prep_body(q_ref, k_ref, wukt_ref, gk_ref, qabs_ref, qab_ref, ss_ref, *, mh, kvr):
    gk = gk_ref[...]
    db = q_ref.shape[0]
    lane = lax.broadcasted_iota(jnp.int32, (db, LANES), 1)
    sgn = jnp.where(lane < 32, 1.0, -1.0)
    cols = []
    for h in range(mh):
        qh = q_ref[:, h * 256:(h + 1) * 256].astype(F32)
        kh = k_ref[:, h * 256:(h + 1) * 256].astype(F32)
        cols.append(jnp.sum(qh * kh, axis=-1, keepdims=True))
        qn = qh[:, :128] * gk[:, :128]
        qabs = _dot(qn, wukt_ref[h], precision=lax.Precision.HIGHEST)
        qabs_ref[:, h * kvr:(h + 1) * kvr] = qabs.astype(qabs_ref.dtype)
        qraw = qh[:, 128:]
        qa = qraw * gk[:, 128:]
        qb = _swap_halves64(qraw, lane) * gk[:, 128:] * sgn
        qab_ref[:, h * 128:(h + 1) * 128] = (qa + pltpu.roll(qb, 64, 1)).astype(qab_ref.dtype)
    ss_ref[...] = jnp.concatenate(cols, axis=-1)


def _sample_attn_body(pt_ref, qabs_ref, qab_ref, ss_ref, cnew_ref, wukt_ref, ctab_ref, stab_ref, lat_hbm, krt_hbm,
                      o_ref, m_scr, l_scr, acc_scr, lat_buf, krt_buf, sem,
                      *, npp, grp, mchunk, mh, nope, qk_dim, page):
    g = pl.program_id(1)
    n_groups = pl.num_programs(1)
    step = pl.program_id(0) * n_groups + g
    n_steps = pl.num_programs(0) * n_groups
    slot = step % 2

    def page_copies(t, sl):
        cps = []
        for r in range(npp):
            pg = pt_ref[t * npp + r]
            cps.append(pltpu.make_async_copy(lat_hbm.at[pg], lat_buf.at[sl, r], sem.at[0, sl]))
            cps.append(pltpu.make_async_copy(krt_hbm.at[pg], krt_buf.at[sl, r], sem.at[1, sl]))
        return cps

    @pl.when(step == 0)
    def _():
        for cp in page_copies(0, 0):
            cp.start()

    @pl.when(step + 1 < n_steps)
    def _():
        for cp in page_copies(step + 1, 1 - slot):
            cp.start()

    for cp in page_copies(step, slot):
        cp.wait()

    @pl.when(g == 0)
    def _():
        m_scr[...] = ss_ref[0]
        l_scr[...] = jnp.ones(l_scr.shape, F32)
        acc_scr[...] = jnp.broadcast_to(cnew_ref[0], acc_scr.shape)

    qabs = qabs_ref[0]
    qab = qab_ref[0]
    n = grp * page
    hpc = mchunk // nope
    s_parts = []
    cgs = []
    for sub in range(npp // grp):
        ids = range(sub * grp, (sub + 1) * grp)
        cg = jnp.concatenate([lat_buf[slot, r].astype(BF16) for r in ids], axis=0)
        krt = jnp.concatenate([krt_buf[slot, r] for r in ids], axis=1)
        cols = pl.ds(sub * n, n)
        ssr = jnp.sum(krt * krt, axis=0, keepdims=True)
        rhs_rope = jnp.concatenate([krt * ctab_ref[:, cols], krt * stab_ref[:, cols]], axis=0).astype(BF16)
        sr = _dot(qab, rhs_rope)
        ss_parts = []
        for mc in range(mh * nope // mchunk):
            kt = _nt(wukt_ref[mc * mchunk:(mc + 1) * mchunk, :], cg)
            ss_parts.append(jnp.sum((kt * kt).reshape(hpc, nope, n), axis=1))
        ss = jnp.concatenate(ss_parts, axis=0) if len(ss_parts) > 1 else ss_parts[0]
        rinv = lax.rsqrt((ss + ssr) * (1.0 / qk_dim) + EPS)
        s_parts.append((_nt(qabs, cg) + sr) * rinv)
        cgs.append(cg)
    s = jnp.concatenate(s_parts, axis=1) if len(s_parts) > 1 else s_parts[0]
    m_old = m_scr[...]
    m_new = jnp.maximum(m_old, jnp.max(s, axis=-1, keepdims=True))
    corr = jnp.exp(m_old - m_new)
    p = jnp.exp(s - m_new)
    l_scr[...] = l_scr[...] * corr + jnp.sum(p, axis=-1, keepdims=True)
    acc = acc_scr[...] * corr
    for sub, cg in enumerate(cgs):
        acc = acc + _dot(p[:, sub * n:(sub + 1) * n].astype(BF16), cg)
    acc_scr[...] = acc
    m_scr[...] = m_new

    @pl.when(g == pl.num_programs(1) - 1)
    def _():
        o_ref[0] = acc_scr[...] / l_scr[...]


def _head_mm_body(x_ref, w_ref, o_ref):
    o_ref[...] = _dot(x_ref[...].astype(BF16), w_ref[0]).astype(o_ref.dtype)


def _merge_body(og_ref, om_ref, wg_ref, wm_ref, ga_ref, gb_ref, o_ref):
    ya = _dot(og_ref[...], wg_ref[...])
    yb = _dot(om_ref[...], wm_ref[...])
    o_ref[...] = (jax.nn.sigmoid(ga_ref[...]) * ya + jax.nn.sigmoid(gb_ref[...]) * yb).astype(o_ref.dtype)


def _peer_select_body(q_ref, sk_ref, s1_ref, s2_ref, st_ref, *, topk):
    q = q_ref[...].astype(BF16)
    sk = sk_ref[0].astype(BF16)
    s1 = _nt(sk[0], q[:, :128])
    s2 = _nt(sk[1], q[:, 128:])

    def top_values(s):
        vals = []
        for _ in range(topk):
            m = jnp.max(s, axis=0, keepdims=True)
            vals.append(m)
            s = jnp.where(s >= m, -jnp.inf, s)
        return vals

    a = top_values(s1)
    b = jnp.concatenate(top_values(s2), axis=0)
    half = topk // 2
    cand = jnp.concatenate([a[0] + b] + [a[r] + b[:half] for r in range(1, half)]
                           + [jnp.concatenate(a[half:], axis=0) + b[0:1]], axis=0)
    best = top_values(cand)
    z = jnp.exp(best[0] - best[0])
    for r in range(1, topk):
        z = z + jnp.exp(best[r] - best[0])
    s1_ref[0] = s1
    s2_ref[0] = s2
    ts = s1.shape[1]
    st_ref[0] = jnp.concatenate([best[topk - 1], best[0] + jnp.log(z), a[0], jnp.zeros((5, ts), F32)], axis=0)


def _peer_dense_body(x_ref, u_ref, v_ref, s1_ref, s2_ref, st_ref, o_ref, p_scr, wt_scr, e2_scr, *, ph, te, n_eblocks):
    j = pl.program_id(1)
    rd = (j + 1) % 2
    wr = j % 2

    @pl.when(j == 0)
    def _():
        o_ref[...] = jnp.zeros(o_ref.shape, F32)
        p_scr[1] = jnp.zeros(p_scr.shape[1:], p_scr.dtype)
        for h in range(ph):
            e2_scr[h] = jnp.exp(s2_ref[h] - (st_ref[h, 1:2, :] - st_ref[h, 2:3, :]))

    tm, d = x_ref.shape
    na = te // LANES
    nlc = tm // LANES
    kc = d // nlc
    jc = jnp.minimum(j, n_eblocks - 1)
    s1_rows = [[s1_ref[h, pl.ds(jc * na + a, 1), :] for h in range(ph)] for a in range(na)]
    e1_rows = [[jnp.exp(s1_rows[a][h] - st_ref[h, 2:3, :]) for h in range(ph)] for a in range(na)]

    ht = None
    for lc in range(nlc):
        ks = slice(lc * kc, (lc + 1) * kc)
        part = _nt(u_ref[:, ks], x_ref[:, ks])
        ht = part if ht is None else ht + part
        ls = slice(lc * LANES, (lc + 1) * LANES)
        for half in range(2):
            ks2 = slice(half * 64, (half + 1) * 64)
            wts = [None] * na
            for h in range(ph):
                s2h = s2_ref[h, ks2, ls]
                e2h = e2_scr[h, ks2, ls]
                tau = st_ref[h, 0:1, ls]
                for a in range(na):
                    term = jnp.where(s2h + s1_rows[a][h][:, ls] >= tau, e2h, 0.0) * e1_rows[a][h][:, ls]
                    wts[a] = term if wts[a] is None else wts[a] + term
            for a in range(na):
                wt_scr[a * LANES + half * 64:a * LANES + (half + 1) * 64, ls] = wts[a]

    p_old = p_scr[rd]
    for lc in range(nlc):
        cs = slice(lc * kc, (lc + 1) * kc)
        o_ref[:, cs] += _dot(p_old, v_ref[:, cs])
        ls = slice(lc * LANES, (lc + 1) * LANES)
        hta = ht[:, ls]
        blk = wt_scr[:, ls] * (0.5 * hta * (1.0 + lax.erf(hta * (2.0 ** -0.5))))
        p_scr[wr, ls, :] = blk.T.astype(p_scr.dtype)


def _peer_dense(hf, u, v, s1t, s2t, stat, row0, nrows, tm, te):
    d = hf.shape[1]
    n_exp = u.shape[0]
    ph, nkeys, _ = s1t.shape
    nb = n_exp // te
    blk0 = row0 // tm
    assert row0 % tm == 0 and nrows % tm == 0
    return pl.pallas_call(
        functools.partial(_peer_dense_body, ph=ph, te=te, n_eblocks=nb),
        grid=(nrows // tm, nb + 1),
        in_specs=[pl.BlockSpec((tm, d), lambda i, j: (i + blk0, 0)),
                  pl.BlockSpec((te, d), lambda i, j: (jnp.minimum(j, nb - 1), 0)),
                  pl.BlockSpec((te, d), lambda i, j: (jnp.maximum(j - 1, 0), 0)),
                  pl.BlockSpec((ph, nkeys, tm), lambda i, j: (0, 0, i + blk0)),
                  pl.BlockSpec((ph, nkeys, tm), lambda i, j: (0, 0, i + blk0)),
                  pl.BlockSpec((ph, 8, tm), lambda i, j: (0, 0, i + blk0))],
        out_specs=pl.BlockSpec((tm, d), lambda i, j: (i, 0)),
        out_shape=jax.ShapeDtypeStruct((nrows, d), F32),
        scratch_shapes=[pltpu.VMEM((2, tm, te), BF16), pltpu.VMEM((te, tm), F32), pltpu.VMEM((ph, nkeys, tm), F32)],
        compiler_params=_cp(("parallel", "arbitrary")),
        name="peer_dense",
    )(hf, u, v, s1t, s2t, stat)


def _add_body(a_ref, b_ref, o_ref):
    o_ref[...] = a_ref[...] + b_ref[...]


def _add_rows(a, b, row0, nrows, tm):
    d = a.shape[1]
    blk0 = row0 // tm
    return pl.pallas_call(
        _add_body,
        grid=(nrows // tm,),
        in_specs=[pl.BlockSpec((tm, d), lambda i: (i + blk0, 0)),
                  pl.BlockSpec((tm, d), lambda i: (i, 0))],
        out_specs=pl.BlockSpec((tm, d), lambda i: (i, 0)),
        out_shape=jax.ShapeDtypeStruct((nrows, d), F32),
        compiler_params=_cp(("parallel",)),
        name="residual_add",
    )(a, b)


def _rope_tables(pos, rope):
    half = rope // 2
    inv = jnp.power(ROPE_THETA, -2.0 * jnp.arange(half, dtype=F32) / rope)
    ang = pos.astype(F32)[:, None] * inv[None, :]
    return jnp.cos(ang), jnp.sin(ang)


def kernel(x_prompt, x_sample, cache_latent, cache_krope, page_table, state_gdn, state_conv, norm_mix_g, w_in,
           gdn_conv_w, gdn_a_log, gdn_dt_bias, gdn_norm_g, gdn_out, mla_q_norm_g, mla_kv_norm_g, mla_w_uq,
           mla_w_uk, mla_w_uv, mla_qk_norm_q, mla_qk_norm_k, mla_out, w_o, norm_ffn_g, peer_w_q, peer_subkeys,
           peer_u, peer_v):
    depth = w_in.shape[0]
    assert depth == 1, "single-layer trunk"
    bsz, seq, d = x_prompt.shape
    db, dseq, _ = x_sample.shape
    assert dseq == 1
    tp = bsz * seq
    t = tp + db
    page = cache_latent.shape[2]
    kvr = cache_latent.shape[3]
    rope = cache_krope.shape[3]
    n_pages = page_table.shape[1]
    past_len = n_pages * page
    gh, dk, dv = state_gdn.shape[2:]
    conv = gdn_conv_w.shape[1]
    cw = gdn_conv_w.shape[2]
    qkw = gh * dk
    vw = gh * dv
    qr = mla_q_norm_g.shape[1]
    mh, nope = mla_w_uk.shape[2:]
    vdim = mla_w_uv.shape[3]
    qk_dim = nope + rope
    ph, _, nkeys, qhalf = peer_subkeys.shape[1:]
    n_exp = peer_u.shape[1]
    assert dk == LANES and dv == LANES and nope == LANES and vdim == LANES and rope == 64
    assert nkeys == LANES and qhalf == LANES and page == LANES and n_exp == nkeys * nkeys
    assert tp % db == 0 and seq % GDN_CHUNK == 0

    hb = _pick(gh, (4, 2, 1))
    ngrp = gh // hb
    assert hb <= 64

    sizes = (cw, gh, gh, vw, qr, kvr, rope, d, d)
    offs = [0]
    for s in sizes:
        offs.append(offs[-1] + s)
    w_in0 = w_in[0]

    def wcol(i):
        return w_in0[:, offs[i]:offs[i + 1]]

    w_big = jnp.concatenate([wcol(0), wcol(3), wcol(7), wcol(8)], axis=1).astype(BF16)
    off_zg, off_ga, off_gb = cw, cw + vw, cw + vw + d
    wa = wcol(1).reshape(d, ngrp, hb)
    wb = wcol(2).reshape(d, ngrp, hb)
    w_ab = jnp.concatenate([wa, jnp.zeros((d, ngrp, 64 - hb), F32), wb, jnp.zeros((d, ngrp, 64 - hb), F32)],
                           axis=2).reshape(d, ngrp * LANES)
    w_small = jnp.concatenate([wcol(4), wcol(5), wcol(6), jnp.zeros((d, LANES - rope), F32), w_ab],
                              axis=1).astype(BF16)

    def group_lanes(vec):
        return jnp.pad(vec.reshape(ngrp, hb), ((0, 0), (0, LANES - hb))).reshape(ngrp * LANES)

    par = jnp.stack([group_lanes(gdn_a_log[0]), group_lanes(gdn_dt_bias[0])], axis=0)

    xp2 = x_prompt.reshape(tp, d)
    xs2 = x_sample.reshape(db, d)
    (xn,) = _rmsnorm_rows2(xp2, xs2, norm_mix_g[0])
    z_big = _matmul(xn, w_big, F32, name="in_proj_big")
    c_q, c_kv_raw, kr_pad, ab = _proj_multi(xn, w_small, (qr, kvr, LANES, ngrp * LANES))

    cwb = _pick(qkw, (512, 256, 128))
    nq_blocks = qkw // cwb
    conv_w = gdn_conv_w[0]
    qkv_act_p, cv_p = pl.pallas_call(
        functools.partial(_conv_prompt_body, nq_blocks=nq_blocks, dk=dk, conv=conv),
        grid=(bsz, cw // cwb),
        in_specs=[pl.BlockSpec((seq, cwb), lambda b, c: (b, c)),
                  pl.BlockSpec((conv, cwb), lambda b, c: (0, c))],
        out_specs=[pl.BlockSpec((seq, cwb), lambda b, c: (b, c)),
                   pl.BlockSpec((1, conv - 1, cwb), lambda b, c: (b, 0, c))],
        out_shape=[jax.ShapeDtypeStruct((tp, cw), BF16),
                   jax.ShapeDtypeStruct((bsz, conv - 1, cw), F32)],
        compiler_params=_cp(("parallel", "parallel")),
        name="gdn_conv_prompt",
    )(z_big, conv_w)

    st_conv2 = state_conv[0].reshape(db, (conv - 1) * cw)
    ncb = cw // cwb
    srow = tp // db
    qkv_act_s = pl.pallas_call(
        functools.partial(_conv_sample_body, nq_blocks=nq_blocks, dk=dk, conv=conv),
        grid=(ncb,),
        in_specs=[pl.BlockSpec((db, cwb), functools.partial(lambda c, j: (0, j * ncb + c), j=j))
                  for j in range(conv - 1)]
                 + [pl.BlockSpec((db, cwb), lambda c: (srow, c)),
                    pl.BlockSpec((conv, cwb), lambda c: (0, c))],
        out_specs=pl.BlockSpec((db, cwb), lambda c: (0, c)),
        out_shape=jax.ShapeDtypeStruct((db, cw), F32),
        compiler_params=_cp(("parallel",)),
        name="gdn_conv_sample",
    )(*([st_conv2] * (conv - 1)), z_big, conv_w)

    hw = hb * dk
    assert off_zg % hw == 0 and qkw % hw == 0
    par3 = par.reshape(2, ngrp, LANES).transpose(1, 0, 2)
    gnorm = gdn_norm_g[0].reshape(1, dv)
    o_gdn_p, s_gdn_p = pl.pallas_call(
        functools.partial(_gdn_prompt_body, hb=hb, chunk=GDN_CHUNK, dk=dk),
        grid=(bsz, ngrp),
        in_specs=[pl.BlockSpec((seq, hw), lambda b, g: (b, g)),
                  pl.BlockSpec((seq, hw), lambda b, g: (b, ngrp + g)),
                  pl.BlockSpec((seq, hw), lambda b, g: (b, 2 * ngrp + g)),
                  pl.BlockSpec((seq, hw), lambda b, g: (b, off_zg // hw + g)),
                  pl.BlockSpec((seq, LANES), lambda b, g: (b, g)),
                  pl.BlockSpec((1, 2, LANES), lambda b, g: (g, 0, 0)),
                  pl.BlockSpec((1, dv), lambda b, g: (0, 0))],
        out_specs=[pl.BlockSpec((seq, hw), lambda b, g: (b, g)),
                   pl.BlockSpec((1, hb, dk, dv), lambda b, g: (b, g, 0, 0))],
        out_shape=[jax.ShapeDtypeStruct((tp, vw), BF16),
                   jax.ShapeDtypeStruct((bsz, gh, dk, dv), F32)],
        scratch_shapes=[pltpu.VMEM((seq, hw), F32), pltpu.VMEM((seq, hw), BF16), pltpu.VMEM((seq, hw), BF16),
                        pltpu.VMEM((seq, hw), BF16), pltpu.VMEM((hb, seq, GDN_CHUNK), BF16),
                        pltpu.VMEM((seq // GDN_CHUNK, LANES), F32)],
        compiler_params=_cp(("parallel", "parallel")),
        name="gdn_prompt",
    )(qkv_act_p, qkv_act_p, qkv_act_p, z_big, ab, par3, gnorm)

    qkv_s3 = qkv_act_s.reshape(db, 3 * gh, dk)
    qkt_s = jnp.swapaxes(qkv_s3[:, :2 * gh, :], 1, 2)
    ab_s = ab[tp:].reshape(db, 1, ngrp * LANES)
    zg_s = z_big[tp:, off_zg:off_zg + vw].reshape(db, 1, vw)
    bb = 1
    o_gdn_s, s_gdn_s = pl.pallas_call(
        functools.partial(_gdn_sample_body, gh=gh, hb=hb, dk=dk),
        grid=(db // bb,),
        in_specs=[pl.BlockSpec((bb, 3 * gh, dk), lambda b: (b, 0, 0)),
                  pl.BlockSpec((bb, dk, 2 * gh), lambda b: (b, 0, 0)),
                  pl.BlockSpec((bb, 1, ngrp * LANES), lambda b: (b, 0, 0)),
                  pl.BlockSpec((2, ngrp * LANES), lambda b: (0, 0)),
                  pl.BlockSpec((bb, 1, vw), lambda b: (b, 0, 0)),
                  pl.BlockSpec((1, dv), lambda b: (0, 0)),
                  pl.BlockSpec((bb, gh, dk, dv), lambda b: (b, 0, 0, 0))],
        out_specs=[pl.BlockSpec((bb, 1, vw), lambda b: (b, 0, 0)),
                   pl.BlockSpec((bb, gh, dk, dv), lambda b: (b, 0, 0, 0))],
        out_shape=[jax.ShapeDtypeStruct((db, 1, vw), BF16),
                   jax.ShapeDtypeStruct((db, gh, dk, dv), F32)],
        compiler_params=_cp(("parallel",)),
        name="gdn_sample",
    )(qkv_s3, qkt_s, ab_s, par, zg_s, gnorm, state_gdn[0].astype(F32))
    o_gdn = jnp.concatenate([o_gdn_p, o_gdn_s.reshape(db, vw)], axis=0)

    pos_all = jnp.concatenate([jnp.tile(jnp.arange(seq, dtype=jnp.int32), bsz),
                               jnp.full((db,), past_len, jnp.int32)])
    cos_h, sin_h = _rope_tables(pos_all, rope)
    zpad = jnp.zeros((t, LANES - rope), F32)
    cos_t = jnp.concatenate([cos_h, cos_h, zpad], axis=1)
    sin_t = jnp.concatenate([-sin_h, sin_h, zpad], axis=1)

    w_uq3 = mla_w_uq[0].reshape(qr, mh, qk_dim)
    w_uq_pad = jnp.concatenate([w_uq3, jnp.zeros((qr, mh, 256 - qk_dim), F32)], axis=2).reshape(qr, mh * 256)
    w_uq_pad = w_uq_pad.astype(BF16)

    def gain256(gvec):
        return jnp.concatenate([gvec, jnp.zeros((256 - qk_dim,), F32)]).reshape(1, 256)

    tmq = _pick(t, (320, 256, 128, 64, 8))
    sm_scale = qk_dim ** -0.5
    q_all = pl.pallas_call(
        functools.partial(_mla_q_body, mh=mh, qk_dim=qk_dim, scale=sm_scale),
        grid=(t // tmq,),
        in_specs=[pl.BlockSpec((tmq, qr), lambda i: (i, 0)),
                  pl.BlockSpec((1, qr), lambda i: (0, 0)),
                  pl.BlockSpec((qr, mh * 256), lambda i: (0, 0)),
                  pl.BlockSpec((1, 256), lambda i: (0, 0)),
                  pl.BlockSpec((tmq, LANES), lambda i: (i, 0)),
                  pl.BlockSpec((tmq, LANES), lambda i: (i, 0))],
        out_specs=pl.BlockSpec((tmq, mh * 256), lambda i: (i, 0)),
        out_shape=jax.ShapeDtypeStruct((t, mh * 256), BF16),
        compiler_params=_cp(("parallel",)),
        name="mla_q",
    )(c_q, mla_q_norm_g[0].reshape(1, qr), w_uq_pad, gain256(mla_qk_norm_q[0]), cos_t, sin_t)

    w_uk2 = mla_w_uk[0].reshape(kvr, mh * nope).astype(BF16)
    w_uv2 = mla_w_uv[0].reshape(kvr, mh * vdim).astype(BF16)
    gk256 = gain256(mla_qk_norm_k[0])
    lat_all, k_all, v_all = pl.pallas_call(
        functools.partial(_mla_k_body, mh=mh, qk_dim=qk_dim),
        grid=(t // tmq,),
        in_specs=[pl.BlockSpec((tmq, kvr), lambda i: (i, 0)),
                  pl.BlockSpec((tmq, LANES), lambda i: (i, 0)),
                  pl.BlockSpec((1, kvr), lambda i: (0, 0)),
                  pl.BlockSpec((kvr, mh * nope), lambda i: (0, 0)),
                  pl.BlockSpec((kvr, mh * vdim), lambda i: (0, 0)),
                  pl.BlockSpec((1, 256), lambda i: (0, 0)),
                  pl.BlockSpec((tmq, LANES), lambda i: (i, 0)),
                  pl.BlockSpec((tmq, LANES), lambda i: (i, 0))],
        out_specs=[pl.BlockSpec((tmq, kvr), lambda i: (i, 0)),
                   pl.BlockSpec((tmq, mh * 256), lambda i: (i, 0)),
                   pl.BlockSpec((tmq, mh * vdim), lambda i: (i, 0))],
        out_shape=[jax.ShapeDtypeStruct((t, kvr), F32),
                   jax.ShapeDtypeStruct((t, mh * 256), BF16),
                   jax.ShapeDtypeStruct((t, mh * vdim), BF16)],
        compiler_params=_cp(("parallel",)),
        name="mla_k",
    )(c_kv_raw, kr_pad, mla_kv_norm_g[0].reshape(1, kvr), w_uk2, w_uv2, gk256, cos_t, sin_t)

    tq = _pick(seq, (512, 256, 128))
    o_mla_p = pl.pallas_call(
        functools.partial(_causal_attn_body, tq=tq),
        grid=(bsz, mh),
        in_specs=[pl.BlockSpec((seq, 256), lambda b, h: (b, h)),
                  pl.BlockSpec((seq, 256), lambda b, h: (b, h)),
                  pl.BlockSpec((seq, vdim), lambda b, h: (b, h))],
        out_specs=pl.BlockSpec((seq, vdim), lambda b, h: (b, h)),
        out_shape=jax.ShapeDtypeStruct((tp, mh * vdim), BF16),
        compiler_params=_cp(("parallel", "parallel")),
        name="mla_attn_prompt",
    )(q_all, k_all, v_all)

    w_ukt = jnp.transpose(mla_w_uk[0], (1, 2, 0))
    q_s = q_all[tp:]
    k_s = k_all[tp:]
    qabs, qab, s_self = pl.pallas_call(
        functools.partial(_sample_prep_body, mh=mh, kvr=kvr),
        out_shape=[jax.ShapeDtypeStruct((db, mh * kvr), BF16),
                   jax.ShapeDtypeStruct((db, mh * LANES), BF16),
                   jax.ShapeDtypeStruct((db, mh), F32)],
        compiler_params=pltpu.CompilerParams(vmem_limit_bytes=VMEM_LIMIT),
        name="mla_sample_prep",
    )(q_s, k_s, w_ukt, gk256)

    npp = _pick(n_pages, (16, 8, 4, 2, 1))
    grp = _pick(npp, (4, 2, 1))
    mchunk = _pick(mh * nope, (512, 256, 128))
    pos_past = jnp.arange(past_len, dtype=jnp.int32)
    cos_p, sin_p = _rope_tables(pos_past, rope)
    ctab = jnp.concatenate([cos_p, cos_p], axis=1).T
    stab = jnp.concatenate([sin_p, sin_p], axis=1).T
    w_ukt2 = w_ukt.reshape(mh * nope, kvr).astype(BF16)
    lat0 = cache_latent[0]
    krt0 = jnp.swapaxes(cache_krope[0], 1, 2)

    lat_s = pl.pallas_call(
        functools.partial(_sample_attn_body, npp=npp, grp=grp, mchunk=mchunk, mh=mh, nope=nope, qk_dim=qk_dim,
                          page=page),
        grid_spec=pltpu.PrefetchScalarGridSpec(
            num_scalar_prefetch=1,
            grid=(db, n_pages // npp),
            in_specs=[pl.BlockSpec((1, mh, kvr), lambda b, g, pt: (b, 0, 0)),
                      pl.BlockSpec((1, mh, LANES), lambda b, g, pt: (b, 0, 0)),
                      pl.BlockSpec((1, mh, 1), lambda b, g, pt: (b, 0, 0)),
                      pl.BlockSpec((1, 1, kvr), lambda b, g, pt: (b, 0, 0)),
                      pl.BlockSpec((mh * nope, kvr), lambda b, g, pt: (0, 0)),
                      pl.BlockSpec((rope, npp * page), lambda b, g, pt: (0, g)),
                      pl.BlockSpec((rope, npp * page), lambda b, g, pt: (0, g)),
                      pl.BlockSpec(memory_space=pl.ANY),
                      pl.BlockSpec(memory_space=pl.ANY)],
            out_specs=pl.BlockSpec((1, mh, kvr), lambda b, g, pt: (b, 0, 0)),
            scratch_shapes=[pltpu.VMEM((mh, 1), F32), pltpu.VMEM((mh, 1), F32), pltpu.VMEM((mh, kvr), F32),
                            pltpu.VMEM((2, npp, page, kvr), F32), pltpu.VMEM((2, npp, rope, page), F32),
                            pltpu.SemaphoreType.DMA((2, 2))],
        ),
        out_shape=jax.ShapeDtypeStruct((db, mh, kvr), F32),
        compiler_params=_cp(("arbitrary", "arbitrary")),
        name="mla_sample_attn",
    )(page_table.reshape(-1).astype(jnp.int32),
      qabs.reshape(db, mh, kvr), qab.reshape(db, mh, LANES), s_self.reshape(db, mh, 1),
      lat_all[tp:].reshape(db, 1, kvr), w_ukt2, ctab, stab, lat0, krt0)

    w_uv3 = jnp.transpose(mla_w_uv[0], (1, 0, 2)).astype(BF16)
    o_mla_s = pl.pallas_call(
        _head_mm_body,
        grid=(mh,),
        in_specs=[pl.BlockSpec((db, kvr), lambda h: (0, h)),
                  pl.BlockSpec((1, kvr, vdim), lambda h: (h, 0, 0))],
        out_specs=pl.BlockSpec((db, vdim), lambda h: (0, h)),
        out_shape=jax.ShapeDtypeStruct((db, mh * vdim), BF16),
        compiler_params=_cp(("parallel",)),
        name="mla_sample_uv",
    )(lat_s.reshape(db, mh * kvr), w_uv3)
    o_mla = jnp.concatenate([o_mla_p, o_mla_s], axis=0)

    tm = _pick(t, (640, 512, 256, 128, 64, 8))
    tn = _pick(d, (512, 256, 128))
    assert off_ga % tn == 0 and off_gb % tn == 0
    merged = pl.pallas_call(
        _merge_body,
        grid=(d // tn, t // tm),
        in_specs=[pl.BlockSpec((tm, vw), lambda j, i: (i, 0)),
                  pl.BlockSpec((tm, mh * vdim), lambda j, i: (i, 0)),
                  pl.BlockSpec((vw, tn), lambda j, i: (0, j)),
                  pl.BlockSpec((mh * vdim, tn), lambda j, i: (0, j)),
                  pl.BlockSpec((tm, tn), lambda j, i: (i, off_ga // tn + j)),
                  pl.BlockSpec((tm, tn), lambda j, i: (i, off_gb // tn + j))],
        out_specs=pl.BlockSpec((tm, tn), lambda j, i: (i, j)),
        out_shape=jax.ShapeDtypeStruct((t, d), BF16),
        compiler_params=_cp(("parallel", "parallel")),
        name="mixer_merge",
    )(o_gdn, o_mla, gdn_out[0].astype(BF16), mla_out[0].astype(BF16), z_big, z_big)
    y_o = _matmul(merged, w_o[0].astype(BF16), F32, name="out_proj")

    x2, hf = _rmsnorm_rows2(xp2, xs2, norm_ffn_g[0], branch=y_o)
    pq = _matmul(hf, peer_w_q[0].astype(BF16), F32, name="peer_query")
    ts = _pick(t, (640, 512, 384, 256, 128))
    s1t, s2t, stat = pl.pallas_call(
        functools.partial(_peer_select_body, topk=PEER_TOPK),
        grid=(t // ts, ph),
        in_specs=[pl.BlockSpec((ts, 2 * qhalf), lambda i, h: (i, h)),
                  pl.BlockSpec((1, 2, nkeys, qhalf), lambda i, h: (h, 0, 0, 0))],
        out_specs=[pl.BlockSpec((1, nkeys, ts), lambda i, h: (h, 0, i)),
                   pl.BlockSpec((1, nkeys, ts), lambda i, h: (h, 0, i)),
                   pl.BlockSpec((1, 8, ts), lambda i, h: (h, 0, i))],
        out_shape=[jax.ShapeDtypeStruct((ph, nkeys, t), F32),
                   jax.ShapeDtypeStruct((ph, nkeys, t), F32),
                   jax.ShapeDtypeStruct((ph, 8, t), F32)],
        compiler_params=_cp(("parallel", "parallel")),
        name="peer_select",
    )(pq, peer_subkeys[0])

    peer_u16 = peer_u[0].astype(BF16)
    peer_v16 = peer_v[0].astype(BF16)
    te = _pick(n_exp, (512, 256, 128))
    tmp = _pick(tp, (512, 256, 128))
    tms = _pick(db, (128,))
    assert tp % tms == 0
    peer_p = _peer_dense(hf, peer_u16, peer_v16, s1t, s2t, stat, 0, tp, tmp, te)
    peer_s = _peer_dense(hf, peer_u16, peer_v16, s1t, s2t, stat, tp, db, tms, te)

    tma = _pick(db, (128, 64, 8))
    assert tp % tma == 0
    y_p = _add_rows(x2, peer_p, 0, tp, tma).reshape(bsz, seq, d)
    y_s = _add_rows(x2, peer_s, tp, db, tma).reshape(db, 1, d)

    lat_p = lat_all[:tp].reshape(1, bsz, seq, kvr)
    lat_sn = lat_all[tp:].reshape(1, db, 1, kvr)
    kr_p = kr_pad[:tp, :rope].reshape(1, bsz, seq, rope)
    kr_s = kr_pad[tp:, :rope].reshape(1, db, 1, rope)
    cv_p = cv_p[None]
    cv_s = jnp.concatenate([state_conv[0][:, 1:, :], z_big[tp:, :cw].reshape(db, 1, cw)], axis=1)[None]
    return (y_p, y_s, lat_p, kr_p, lat_sn, kr_s,
            s_gdn_p[None].astype(state_gdn.dtype), s_gdn_s[None].astype(state_gdn.dtype), cv_p, cv_s)
```
